```python
import jax
import jax.numpy as jnp
from jax import lax
import numpy as np

D_MODEL = 4096
BATCH = 4
SEQ = 2048
DEPTH = 1
DEC_BATCH = 32
DEC_SEQ = 64
PAST_LEN = 1024

CHUNK = 64
Q_BLOCK = 128
M_HEADS = 8
M_DQK = 256
M_DV = 512
A_HEADS = 32
A_Q_RANK = 1024
A_KV_RANK = 512
A_NOPE = 128
A_ROPE = 64
A_DV = 128
ROPE_THETA = 10000.0
ATTN_SCALE = (A_NOPE + A_ROPE) ** -0.5
PEER_HEADS = 8
PEER_NKEYS = 128
PEER_EXPERTS = PEER_NKEYS * PEER_NKEYS
PEER_DKEY = 256
PEER_HALF = PEER_DKEY // 2
PEER_TOPK = 16
TOK_BLOCK = 64
NORM_EPS = 1e-6
IN_SIZES = (M_HEADS * M_DQK, M_HEADS * M_DQK, M_HEADS * M_DV, M_HEADS * M_DV, M_HEADS, M_HEADS,
            A_Q_RANK, A_KV_RANK, A_ROPE, D_MODEL, D_MODEL)
IN_WIDTH = sum(IN_SIZES)

kernel_name = 'hybrid_mlstm_mla_peer_stream_step'


def rmsnorm(x, g):
    xf = x.astype(jnp.float32)
    y = xf * lax.rsqrt(jnp.mean(xf * xf, axis=-1, keepdims=True) + NORM_EPS)
    return (y * g.astype(jnp.float32)).astype(x.dtype)


def adaln(c, w, b):
    mod = jnp.einsum('bd,de->be', jax.nn.silu(c), w) + b
    return jnp.split(mod[:, None, :], 6, axis=-1)


def rope(x, pos):
    half = A_ROPE // 2
    inv = ROPE_THETA ** (-jnp.arange(half, dtype=jnp.float32) / half)
    ang = pos.astype(jnp.float32)[:, None] * inv[None, :]
    shape = (1, ang.shape[0]) + (1,) * (x.ndim - 3) + (half,)
    cos = jnp.cos(ang).reshape(shape)
    sin = jnp.sin(ang).reshape(shape)
    xf = x.astype(jnp.float32)
    x1, x2 = xf[..., :half], xf[..., half:]
    return jnp.concatenate([x1 * cos - x2 * sin, x2 * cos + x1 * sin], axis=-1).astype(x.dtype)


def mlstm_chunk(state, inputs):
    C, n, m = (t.astype(jnp.float32) for t in state)
    q, k, v, ig, lf = inputs
    L = q.shape[1]
    qf, kf, vf = q.astype(jnp.float32), k.astype(jnp.float32), v.astype(jnp.float32)
    b = jnp.cumsum(lf, axis=1)
    a = b + m[:, None, :]
    causal = jnp.tril(jnp.ones((L, L), dtype=bool))
    dmat = jnp.where(causal[None, :, :, None],
                     b[:, :, None, :] - b[:, None, :, :] + ig[:, None, :, :], -jnp.inf)
    m_t = jnp.maximum(a, jnp.max(dmat, axis=2))
    w_inter = jnp.exp(a - m_t)
    w_intra = jnp.exp(dmat - m_t[:, :, None, :])
    s = jnp.einsum('bthd,bshd->btsh', qf, kf) * w_intra
    num = jnp.einsum('btsh,bshv->bthv', s, vf) + w_inter[..., None] * jnp.einsum('bthd,bhdv->bthv', qf, C)
    den = jnp.sum(s, axis=2) + w_inter * jnp.einsum('bthd,bhd->bth', qf, n)
    h = num / jnp.maximum(jnp.abs(den), jnp.exp(-m_t))[..., None]
    m_new = m_t[:, -1]
    w_c = jnp.exp(b[:, -1] + m - m_new)
    w_s = jnp.exp(b[:, -1:, :] - b + ig - m_new[:, None, :])
    kw = kf * w_s[..., None]
    C_new = w_c[..., None, None] * C + jnp.einsum('bshd,bshv->bhdv', kw, vf)
    n_new = w_c[..., None] * n + jnp.sum(kw, axis=1)
    return (C_new, n_new, m_new), h.astype(q.dtype)


def attend(q, k, v, q_pos, k_pos, shared):
    sub = 'bkd' if shared else 'bkhd'
    s = jnp.einsum(f'bqhd,{sub}->bhqk', q, k).astype(jnp.float32) * ATTN_SCALE
    allowed = (k_pos[None, :] // CHUNK) <= (q_pos[:, None] // CHUNK)
    s = jnp.where(allowed[None, None], s, -jnp.inf)
    p = jax.nn.softmax(s, axis=-1).astype(v.dtype)
    return jnp.einsum(f'bhqk,{sub}->bqhd', p, v)


def token_mixer(h, pos, lp, past):
    B, S, _ = h.shape
    z = jnp.einsum('bsd,de->bse', h, lp['w_in'])
    offsets = np.cumsum(IN_SIZES)[:-1].tolist()
    mq, mk, mv, mo, mi, mf, cq, ckv, kr, za, zb = jnp.split(z, offsets, axis=-1)

    q = mq.reshape(B, S, M_HEADS, M_DQK) * (M_DQK ** -0.5)
    k = mk.reshape(B, S, M_HEADS, M_DQK)
    v = mv.reshape(B, S, M_HEADS, M_DV)
    ig = (mi + lp['b_igate']).astype(jnp.float32)
    lf = jax.nn.log_sigmoid((mf + lp['b_fgate']).astype(jnp.float32))
    if past is None:
        nc = S // CHUNK

        def to_chunks(t):
            return t.reshape((B, nc, CHUNK) + t.shape[2:]).swapaxes(0, 1)

        state0 = (jnp.zeros((B, M_HEADS, M_DQK, M_DV), jnp.float32),
                  jnp.zeros((B, M_HEADS, M_DQK), jnp.float32),
                  jnp.zeros((B, M_HEADS), jnp.float32))
        (C1, n1, m1), hc = lax.scan(mlstm_chunk, state0,
                                    (to_chunks(q), to_chunks(k), to_chunks(v), to_chunks(ig), to_chunks(lf)))
        hm = hc.swapaxes(0, 1).reshape(B, S, M_HEADS, M_DV)
    else:
        p_ckv, p_kr, C0, n0, m0 = past
        (C1, n1, m1), hm = mlstm_chunk((C0, n0, m0), (q, k, v, ig, lf))
    hm = rmsnorm(hm, lp['mlstm_norm_g'].reshape(M_HEADS, M_DV)) * jax.nn.sigmoid(mo).reshape(B, S, M_HEADS, M_DV)
    ya = jnp.einsum('bsf,fd->bsd', hm.reshape(B, S, M_HEADS * M_DV), lp['w_branch_m'])

    cq = rmsnorm(cq, lp['q_norm_g'])
    qh = jnp.einsum('bsr,rhe->bshe', cq, lp['w_uq'])
    q_nope = qh[..., :A_NOPE]
    q_rope = rope(qh[..., A_NOPE:], pos)
    ckv = rmsnorm(ckv, lp['kv_norm_g'])
    krope = rope(kr, pos)
    if past is None:
        k_nope = jnp.einsum('bsc,chn->bshn', ckv, lp['w_uk'])
        keys = jnp.concatenate([k_nope, jnp.broadcast_to(krope[:, :, None, :], (B, S, A_HEADS, A_ROPE))], axis=-1)
        vals = jnp.einsum('bsc,chv->bshv', ckv, lp['w_uv'])
        queries = jnp.concatenate([q_nope, q_rope], axis=-1)
        nb = S // Q_BLOCK
        qb = queries.reshape(B, nb, Q_BLOCK, A_HEADS, A_NOPE + A_ROPE).swapaxes(0, 1)
        pb = pos.reshape(nb, Q_BLOCK)
        ob = lax.map(lambda qp: attend(qp[0], keys, vals, qp[1], pos, False), (qb, pb))
        attn = ob.swapaxes(0, 1).reshape(B, S, A_HEADS, A_DV)
    else:
        ckv_all = jnp.concatenate([p_ckv.astype(ckv.dtype), ckv], axis=1)
        kr_all = jnp.concatenate([p_kr.astype(krope.dtype), krope], axis=1)
        k_pos = jnp.arange(ckv_all.shape[1])
        q_lat = jnp.einsum('bshn,chn->bshc', q_nope, lp['w_uk'])
        queries = jnp.concatenate([q_lat, q_rope], axis=-1)
        keys = jnp.concatenate([ckv_all, kr_all], axis=-1)
        o_lat = attend(queries, keys, ckv_all, pos, k_pos, True)
        attn = jnp.einsum('bshc,chv->bshv', o_lat, lp['w_uv'])
    yb = jnp.einsum('bsf,fd->bsd', attn.reshape(B, S, A_HEADS * A_DV), lp['w_branch_a'])

    y = jax.nn.sigmoid(za) * ya + jax.nn.sigmoid(zb) * yb
    y = jnp.einsum('bsd,de->bse', y, lp['w_out'])
    return y, (ckv, krope, C1, n1, m1)


def peer(h, wq, subkeys, u_tab, v_tab):
    B, S, D = h.shape
    hf = h.reshape(B * S, D)
    T = hf.shape[0]
    q = jnp.einsum('td,de->te', hf, wq).reshape(T, PEER_HEADS, 2, PEER_HALF)
    s = jnp.einsum('thpd,hpnd->thpn', q, subkeys).astype(jnp.float32)
    s_top, i_top = lax.top_k(s, PEER_TOPK)
    cand_s = (s_top[:, :, 0, :, None] + s_top[:, :, 1, None, :]).reshape(T, PEER_HEADS, PEER_TOPK * PEER_TOPK)
    cand_i = (i_top[:, :, 0, :, None] * PEER_NKEYS + i_top[:, :, 1, None, :]).reshape(T, PEER_HEADS, PEER_TOPK * PEER_TOPK)
    best_s, best_j = lax.top_k(cand_s, PEER_TOPK)
    idx = jnp.take_along_axis(cand_i, best_j, axis=-1)
    g = jax.nn.softmax(best_s, axis=-1)
    pad = (-T) % TOK_BLOCK
    nb = (T + pad) // TOK_BLOCK
    xb = jnp.pad(hf, ((0, pad), (0, 0))).reshape(nb, TOK_BLOCK, D)
    ib = jnp.pad(idx, ((0, pad), (0, 0), (0, 0))).reshape(nb, TOK_BLOCK, PEER_HEADS, PEER_TOPK)
    gb = jnp.pad(g, ((0, pad), (0, 0), (0, 0))).reshape(nb, TOK_BLOCK, PEER_HEADS, PEER_TOPK)

    def block(args):
        xt, it, gt = args
        act = jax.nn.gelu(jnp.einsum('thkd,td->thk', u_tab[it], xt).astype(jnp.float32), approximate=False)
        coef = (gt * act).astype(xt.dtype)
        return jnp.einsum('thk,thkd->td', coef, v_tab[it])

    out = lax.map(block, (xb, ib, gb)).reshape(nb * TOK_BLOCK, D)[:T]
    return out.reshape(B, S, D)


def layer(x, c, pos, lp, past):
    sh1, sc1, g1, sh2, sc2, g2 = adaln(c, lp['ada_w'], lp['ada_b'])
    h = rmsnorm(x, lp['norm1_g']) * (1 + sc1) + sh1
    y, new_state = token_mixer(h, pos, lp, past)
    x = x + g1 * y
    h = rmsnorm(x, lp['norm2_g']) * (1 + sc2) + sh2
    x = x + g2 * peer(h, lp['peer_wq'], lp['peer_subkeys'], lp['peer_u'], lp['peer_v'])
    return x, new_state


def setup_inputs(seed: int = 0) -> dict:
    key = jax.random.key(seed)
    ks = iter(jax.random.split(key, 40))

    def nrm(shape, scale):
        return jax.random.normal(next(ks), shape, jnp.float32) * scale

    D = D_MODEL
    return {
        'x_prompt': nrm((BATCH, SEQ, D), 1.0),
        'x_sample': nrm((DEC_BATCH, DEC_SEQ, D), 1.0),
        'c_prompt': nrm((BATCH, D), 1.0),
        'c_sample': nrm((DEC_BATCH, D), 1.0),
        'cache_ckv': nrm((DEPTH, DEC_BATCH, PAST_LEN, A_KV_RANK), 1.0),
        'cache_krope': nrm((DEPTH, DEC_BATCH, PAST_LEN, A_ROPE), 1.0),
        'state_C': nrm((DEPTH, DEC_BATCH, M_HEADS, M_DQK, M_DV), 0.5),
        'state_n': nrm((DEPTH, DEC_BATCH, M_HEADS, M_DQK), 1.0),
        'state_m': nrm((DEPTH, DEC_BATCH, M_HEADS), 1.0),
        'ada_w': nrm((DEPTH, D, 6 * D), 0.5 * D ** -0.5),
        'ada_b': nrm((DEPTH, 6 * D), 0.02),
        'norm1_g': 1.0 + nrm((DEPTH, D), 0.02),
        'w_in': nrm((DEPTH, D, IN_WIDTH), D ** -0.5),
        'b_igate': nrm((DEPTH, M_HEADS), 0.1),
        'b_fgate': 3.0 + nrm((DEPTH, M_HEADS), 0.5),
        'mlstm_norm_g': 1.0 + nrm((DEPTH, M_HEADS * M_DV), 0.02),
        'q_norm_g': 1.0 + nrm((DEPTH, A_Q_RANK), 0.02),
        'kv_norm_g': 1.0 + nrm((DEPTH, A_KV_RANK), 0.02),
        'w_uq': nrm((DEPTH, A_Q_RANK, A_HEADS, A_NOPE + A_ROPE), A_Q_RANK ** -0.5),
        'w_uk': nrm((DEPTH, A_KV_RANK, A_HEADS, A_NOPE), A_KV_RANK ** -0.5),
        'w_uv': nrm((DEPTH, A_KV_RANK, A_HEADS, A_DV), A_KV_RANK ** -0.5),
        'w_branch_m': nrm((DEPTH, M_HEADS * M_DV, D), (M_HEADS * M_DV) ** -0.5),
        'w_branch_a': nrm((DEPTH, A_HEADS * A_DV, D), (A_HEADS * A_DV) ** -0.5),
        'w_out': nrm((DEPTH, D, D), D ** -0.5),
        'norm2_g': 1.0 + nrm((DEPTH, D), 0.02),
        'peer_wq': nrm((DEPTH, D, PEER_HEADS * PEER_DKEY), D ** -0.5),
        'peer_subkeys': nrm((DEPTH, PEER_HEADS, 2, PEER_NKEYS, PEER_HALF), PEER_HALF ** -0.5),
        'peer_u': nrm((DEPTH, PEER_EXPERTS, D), D ** -0.5),
        'peer_v': nrm((DEPTH, PEER_EXPERTS, D), PEER_HEADS ** -0.5),
        'final_norm_g': 1.0 + nrm((D,), 0.02),
    }


def reference(x_prompt, x_sample, c_prompt, c_sample, cache_ckv, cache_krope, state_C, state_n, state_m,
              ada_w, ada_b, norm1_g, w_in, b_igate, b_fgate, mlstm_norm_g, q_norm_g, kv_norm_g,
              w_uq, w_uk, w_uv, w_branch_m, w_branch_a, w_out, norm2_g, peer_wq, peer_subkeys,
              peer_u, peer_v, final_norm_g):
    past_len = cache_ckv.shape[2]
    pos_p = jnp.arange(x_prompt.shape[1])
    pos_s = past_len + jnp.arange(x_sample.shape[1])
    xp, xs = x_prompt, x_sample
    p_states, s_states = [], []
    for l in range(DEPTH):
        lp = {
            'ada_w': ada_w[l], 'ada_b': ada_b[l], 'norm1_g': norm1_g[l], 'w_in': w_in[l],
            'b_igate': b_igate[l], 'b_fgate': b_fgate[l], 'mlstm_norm_g': mlstm_norm_g[l],
            'q_norm_g': q_norm_g[l], 'kv_norm_g': kv_norm_g[l], 'w_uq': w_uq[l], 'w_uk': w_uk[l],
            'w_uv': w_uv[l], 'w_branch_m': w_branch_m[l], 'w_branch_a': w_branch_a[l], 'w_out': w_out[l],
            'norm2_g': norm2_g[l], 'peer_wq': peer_wq[l], 'peer_subkeys': peer_subkeys[l],
            'peer_u': peer_u[l], 'peer_v': peer_v[l],
        }
        xp, st_p = layer(xp, c_prompt, pos_p, lp, None)
        xs, st_s = layer(xs, c_sample, pos_s, lp,
                         (cache_ckv[l], cache_krope[l], state_C[l], state_n[l], state_m[l]))
        p_states.append(st_p)
        s_states.append(st_s)
    y_prompt = rmsnorm(xp, final_norm_g)
    y_sample = rmsnorm(xs, final_norm_g)
    p_ckv, p_kr, p_C, p_n, p_m = [jnp.stack(t) for t in zip(*p_states)]
    s_ckv, s_kr, s_C, s_n, s_m = [jnp.stack(t) for t in zip(*s_states)]
    return (y_prompt, y_sample, p_ckv, p_kr, p_C, p_n, p_m, s_ckv, s_kr, s_C, s_n, s_m)
```

```python
import functools
import math

import jax
import jax.numpy as jnp
import numpy as np
from jax import lax
from jax.experimental import pallas as pl
from jax.experimental.pallas import tpu as pltpu

CHUNK = 64
A_NOPE = 128
A_ROPE = 64
A_DV = 128
ROPE_THETA = 10000.0
PEER_NKEYS = 128
PEER_TOPK = 16
NORM_EPS = 1e-6

F32 = jnp.float32
BF16 = jnp.bfloat16
MIB = 1024 * 1024
NEG_INF = float("-inf")


def _cparams(sem, vmem_mib=48):
    return pltpu.CompilerParams(dimension_semantics=sem, vmem_limit_bytes=vmem_mib * MIB)


def _tile(n, want, mult=1):
    if n <= want:
        return n
    t = (want // mult) * mult
    while t >= mult:
        if n % t == 0:
            return t
        t -= mult
    return n


def _dot(a, b):
    return jnp.dot(a, b, preferred_element_type=F32)


def _dot_nt(a, b):
    return lax.dot_general(a, b, (((1,), (1,)), ((), ())), preferred_element_type=F32)


def _silu(x):
    return x * jax.nn.sigmoid(x)


def _mm_body(*refs, nk, n_extra, a_fn, epilogue):
    a_ref, w_ref = refs[0], refs[1]
    extras = refs[2:2 + n_extra]
    o_ref = refs[2 + n_extra]
    a = a_ref[...]
    if a_fn is not None:
        a = a_fn(a)
    part = _dot(a.astype(BF16), w_ref[...].astype(BF16))

    def finish(acc):
        res = acc if epilogue is None else epilogue(acc, *[e[...] for e in extras])
        o_ref[...] = res.astype(o_ref.dtype)

    if nk == 1:
        finish(part)
        return
    acc_ref = refs[3 + n_extra]
    k = pl.program_id(2)

    @pl.when(k == 0)
    def _():
        acc_ref[...] = part

    @pl.when(k > 0)
    def _():
        acc_ref[...] += part

    @pl.when(k == nk - 1)
    def _():
        finish(acc_ref[...])


def _mm(a, w, *, out_dtype, tm=1024, tn=1024, tk=2048, extras=(), epilogue=None, a_fn=None,
        w_col0=0, n_out=None, vmem_mib=48, name="mm"):
    m, kdim = a.shape
    n_out = w.shape[1] if n_out is None else n_out
    tm = _tile(m, tm, 8)
    tn = _tile(n_out, tn, 128)
    tk = _tile(kdim, tk, 128)
    assert w_col0 % tn == 0
    jb0 = w_col0 // tn
    nk = kdim // tk
    grid = (m // tm, n_out // tn, nk)
    in_specs = [pl.BlockSpec((tm, tk), lambda i, j, k: (i, k)),
                pl.BlockSpec((tk, tn), lambda i, j, k: (k, j + jb0))]
    args = [a, w]
    for arr, bshape, imap in extras:
        in_specs.append(pl.BlockSpec(bshape, functools.partial(lambda i, j, k, f: f(i, j), f=imap)))
        args.append(arr)
    scratch = [pltpu.VMEM((tm, tn), F32)] if nk > 1 else []
    return pl.pallas_call(
        functools.partial(_mm_body, nk=nk, n_extra=len(extras), a_fn=a_fn, epilogue=epilogue),
        grid=grid,
        in_specs=in_specs,
        out_specs=pl.BlockSpec((tm, tn), lambda i, j, k: (i, j)),
        out_shape=jax.ShapeDtypeStruct((m, n_out), out_dtype),
        scratch_shapes=scratch,
        compiler_params=_cparams(("parallel", "parallel", "arbitrary"), vmem_mib),
        name=name,
    )(*args)


def _mmh_body(*refs, n_extra, epilogue):
    a_ref, w_ref = refs[0], refs[1]
    extras = refs[2:2 + n_extra]
    o_ref = refs[2 + n_extra]
    acc = _dot(a_ref[...], w_ref[0])
    res = acc if epilogue is None else epilogue(acc, *[e[...] for e in extras])
    o_ref[0] = res.astype(o_ref.dtype)


def _mm_heads(a, w, *, n_out, out_dtype, tm=1024, extras=(), epilogue=None, name="mm_heads"):
    m, kdim = a.shape
    nh, _, n = w.shape
    tm = _tile(m, tm, 8)
    in_specs = [pl.BlockSpec((tm, kdim), lambda i, h: (i, 0)),
                pl.BlockSpec((1, kdim, n), lambda i, h: (h, 0, 0))]
    args = [a, w]
    for arr, bshape, imap in extras:
        in_specs.append(pl.BlockSpec(bshape, imap))
        args.append(arr)
    return pl.pallas_call(
        functools.partial(_mmh_body, n_extra=len(extras), epilogue=epilogue),
        grid=(m // tm, nh),
        in_specs=in_specs,
        out_specs=pl.BlockSpec((1, tm, n_out), lambda i, h: (h, i, 0)),
        out_shape=jax.ShapeDtypeStruct((nh, m, n_out), out_dtype),
        compiler_params=_cparams(("parallel", "arbitrary")),
        name=name,
    )(*args)


def _rms(x):
    return x * lax.rsqrt(jnp.mean(x * x, axis=-1, keepdims=True) + NORM_EPS)


def _norm_mod_body(x_ref, g_ref, sh_ref, sc_ref, o_ref):
    tr, d = x_ref.shape
    y = _rms(x_ref[...]) * g_ref[...]
    y = y.reshape(tr // CHUNK, CHUNK, d) * (1.0 + sc_ref[...]) + sh_ref[...]
    o_ref[...] = y.reshape(tr, d).astype(o_ref.dtype)


def _norm_mod(x, g, modg, shift_idx, scale_idx, *, tr):
    t, d = x.shape
    gpt = tr // CHUNK
    return pl.pallas_call(
        _norm_mod_body,
        grid=(t // tr,),
        in_specs=[pl.BlockSpec((tr, d), lambda i: (i, 0)),
                  pl.BlockSpec((1, d), lambda i: (0, 0)),
                  pl.BlockSpec((gpt, 1, d), lambda i: (i, 0, shift_idx)),
                  pl.BlockSpec((gpt, 1, d), lambda i: (i, 0, scale_idx))],
        out_specs=pl.BlockSpec((tr, d), lambda i: (i, 0)),
        out_shape=jax.ShapeDtypeStruct((t, d), BF16),
        compiler_params=_cparams(("parallel",)),
        name="norm_mod",
    )(x, g, modg, modg)


def _rope64(x, cos, sin_signed):
    half = A_ROPE // 2
    rolled = jnp.concatenate([x[:, half:], x[:, :half]], axis=-1)
    return x * cos + rolled * sin_signed


def _log_sigmoid(x):
    return jnp.minimum(x, 0.0) - jnp.log1p(jnp.exp(-jnp.abs(x)))


def _small_post_body(z_ref, qg_ref, kvg_ref, cos_ref, sin_ref, bias_ref,
                     cq_ref, ckv_ref, kr_ref, gate_ref, *, qr, cr, mh):
    z = z_ref[...]
    cq_ref[...] = (_rms(z[:, :qr]) * qg_ref[...]).astype(cq_ref.dtype)
    ckv_ref[...] = _rms(z[:, qr:qr + cr]) * kvg_ref[...]
    o = qr + cr
    kr_ref[...] = _rope64(z[:, o:o + A_ROPE], cos_ref[...], sin_ref[...])
    o += A_ROPE
    pre = z[:, o:o + 2 * mh] + bias_ref[...]
    lane = lax.broadcasted_iota(jnp.int32, pre.shape, 1)
    gate_ref[...] = jnp.where(lane < mh, pre, _log_sigmoid(pre))


def _small_post(zs, qg, kvg, cos, sin_s, gate_bias, *, qr, cr, mh, tr):
    t, w = zs.shape
    row = lambda i: (i, 0)
    fix = lambda i: (0, 0)
    return pl.pallas_call(
        functools.partial(_small_post_body, qr=qr, cr=cr, mh=mh),
        grid=(t // tr,),
        in_specs=[pl.BlockSpec((tr, w), row), pl.BlockSpec((1, qr), fix), pl.BlockSpec((1, cr), fix),
                  pl.BlockSpec((tr, A_ROPE), row), pl.BlockSpec((tr, A_ROPE), row),
                  pl.BlockSpec((1, 2 * mh), fix)],
        out_specs=[pl.BlockSpec((tr, qr), row), pl.BlockSpec((tr, cr), row),
                   pl.BlockSpec((tr, A_ROPE), row), pl.BlockSpec((tr, 2 * mh), row)],
        out_shape=[jax.ShapeDtypeStruct((t, qr), BF16), jax.ShapeDtypeStruct((t, cr), F32),
                   jax.ShapeDtypeStruct((t, A_ROPE), F32), jax.ShapeDtypeStruct((t, 2 * mh), F32)],
        compiler_params=_cparams(("parallel",)),
        name="small_post",
    )(zs, qg, kvg, cos, sin_s, gate_bias)


def _mlstm_body(*refs, has_state, mh, scale):
    if has_state:
        (q_ref, k_ref, v_ref, mo_ref, gate_ref, g_ref, c0_ref, n0_ref, m0_ref,
         h_ref, c_ref, n_ref, m_ref) = refs
    else:
        (q_ref, k_ref, v_ref, mo_ref, gate_ref, g_ref, h_ref, c_ref, n_ref, m_ref) = refs
    hd = pl.program_id(1)
    ci = pl.program_id(2)
    L = q_ref.shape[0]

    @pl.when(ci == 0)
    def _():
        if has_state:
            c_ref[...] = c0_ref[...]
            n_ref[...] = n0_ref[...]
            m_ref[...] = m0_ref[...]
        else:
            c_ref[...] = jnp.zeros_like(c_ref)
            n_ref[...] = jnp.zeros_like(n_ref)
            m_ref[...] = jnp.zeros_like(m_ref)

    gates = gate_ref[...]
    lane = lax.broadcasted_iota(jnp.int32, gates.shape, 1)
    ig_col = jnp.sum(jnp.where(lane == hd, gates, 0.0), axis=1, keepdims=True)
    lf_col = jnp.sum(jnp.where(lane == hd + mh, gates, 0.0), axis=1, keepdims=True)
    r = lax.broadcasted_iota(jnp.int32, (L, L), 0)
    c = lax.broadcasted_iota(jnp.int32, (L, L), 1)
    eye = r == c
    ig_row = jnp.sum(jnp.where(eye, ig_col, 0.0), axis=0, keepdims=True)
    lf_row = jnp.sum(jnp.where(eye, lf_col, 0.0), axis=0, keepdims=True)
    causal = c <= r
    b_col = jnp.sum(jnp.where(causal, lf_row, 0.0), axis=1, keepdims=True)
    b_row = jnp.sum(jnp.where(r <= c, lf_col, 0.0), axis=0, keepdims=True)
    b_last = jnp.sum(lf_col, axis=0, keepdims=True)

    m_prev = m_ref[0, 0]
    c_prev = c_ref[0, 0]
    n_prev = n_ref[0, 0]
    a_col = b_col + m_prev
    dmat = jnp.where(causal, b_col - b_row + ig_row, NEG_INF)
    m_t = jnp.maximum(a_col, jnp.max(dmat, axis=1, keepdims=True))
    w_inter = jnp.exp(a_col - m_t)
    w_intra = jnp.exp(dmat - m_t)

    qs = (q_ref[...].astype(F32) * scale)
    qb = qs.astype(BF16)
    kf = k_ref[...].astype(F32)
    vb = v_ref[...]
    s = _dot_nt(qb, k_ref[...]) * w_intra
    num = _dot(s.astype(BF16), vb) + w_inter * _dot(qb, c_prev.astype(BF16))
    den = jnp.sum(s, axis=1, keepdims=True) + w_inter * jnp.sum(qs * n_prev, axis=1, keepdims=True)
    h = num / jnp.maximum(jnp.abs(den), jnp.exp(-m_t))
    hn = _rms(h) * g_ref[0]
    h_ref[...] = (hn * jax.nn.sigmoid(mo_ref[...].astype(F32))).astype(h_ref.dtype)

    m_new = jnp.sum(jnp.where(lax.broadcasted_iota(jnp.int32, (L, 1), 0) == L - 1, m_t, 0.0),
                    axis=0, keepdims=True)
    w_c = jnp.exp(b_last + m_prev - m_new)
    w_s = jnp.exp(b_last - b_col + ig_col - m_new)
    kw = kf * w_s
    c_ref[0, 0] = w_c * c_prev + _dot(kw.T.astype(BF16), vb)
    n_ref[0, 0] = w_c * n_prev + jnp.sum(kw, axis=0, keepdims=True)
    m_ref[0, 0] = m_new


def _mlstm(zb, gates, norm_g, state, *, row_blk0, nb, nc, mh, dk, dv, col_q, col_k, col_v, col_o):
    has_state = state is not None
    rowmap = lambda off: (lambda b, h, c: (row_blk0 + b * nc + c, off + h))
    in_specs = [pl.BlockSpec((CHUNK, dk), rowmap(col_q // dk)),
                pl.BlockSpec((CHUNK, dk), rowmap(col_k // dk)),
                pl.BlockSpec((CHUNK, dv), rowmap(col_v // dv)),
                pl.BlockSpec((CHUNK, dv), rowmap(col_o // dv)),
                pl.BlockSpec((CHUNK, 2 * mh), lambda b, h, c: (row_blk0 + b * nc + c, 0)),
                pl.BlockSpec((1, 1, dv), lambda b, h, c: (h, 0, 0))]
    args = [zb, zb, zb, zb, gates, norm_g]
    st_specs = [pl.BlockSpec((1, 1, dk, dv), lambda b, h, c: (b, h, 0, 0)),
                pl.BlockSpec((1, 1, 1, dk), lambda b, h, c: (b, h, 0, 0)),
                pl.BlockSpec((1, 1, 1, 1), lambda b, h, c: (b, h, 0, 0))]
    if has_state:
        in_specs += st_specs
        args += list(state)
    return pl.pallas_call(
        functools.partial(_mlstm_body, has_state=has_state, mh=mh, scale=dk ** -0.5),
        grid=(nb, mh, nc),
        in_specs=in_specs,
        out_specs=[pl.BlockSpec((CHUNK, dv), lambda b, h, c: (b * nc + c, h))] + st_specs,
        out_shape=[jax.ShapeDtypeStruct((nb * nc * CHUNK, mh * dv), BF16),
                   jax.ShapeDtypeStruct((nb, mh, dk, dv), F32),
                   jax.ShapeDtypeStruct((nb, mh, 1, dk), F32),
                   jax.ShapeDtypeStruct((nb, mh, 1, 1), F32)],
        compiler_params=_cparams(("parallel", "parallel", "arbitrary")),
        name="mlstm",
    )(*args)


def _attn_prompt_body(q_ref, k_ref, v_ref, o_ref, *, tq, scale):
    qi = pl.program_id(2)
    q = q_ref[0]
    qpos = qi * tq + lax.broadcasted_iota(jnp.int32, (tq, tq), 0)
    kofs = lax.broadcasted_iota(jnp.int32, (tq, tq), 1)

    def body(kb, carry):
        m, l, acc = carry
        k = k_ref[0, pl.ds(pl.multiple_of(kb * tq, tq), tq), :]
        v = v_ref[0, pl.ds(pl.multiple_of(kb * tq, tq), tq), :]
        s = _dot_nt(q, k) * scale
        allowed = ((kb * tq + kofs) // CHUNK) <= (qpos // CHUNK)
        s = jnp.where(allowed, s, NEG_INF)
        m_new = jnp.maximum(m, jnp.max(s, axis=1, keepdims=True))
        alpha = jnp.exp(m - m_new)
        p = jnp.exp(s - m_new)
        l = alpha * l + jnp.sum(p, axis=1, keepdims=True)
        acc = alpha * acc + _dot(p.astype(BF16), v)
        return m_new, l, acc

    init = (jnp.full((tq, 1), NEG_INF, F32), jnp.zeros((tq, 1), F32), jnp.zeros((tq, v_ref.shape[2]), F32))
    m, l, acc = lax.fori_loop(0, qi + 1, body, init)
    o_ref[...] = (acc / l).astype(o_ref.dtype)


def _attn_prompt(q_hm, k_hm, v_hm, *, nb, seq, tq=256):
    nh, _, dqk = q_hm.shape
    dv = v_hm.shape[2]
    tq = _tile(seq, tq, CHUNK)
    nq = seq // tq
    return pl.pallas_call(
        functools.partial(_attn_prompt_body, tq=tq, scale=dqk ** -0.5),
        grid=(nb, nh, nq),
        in_specs=[pl.BlockSpec((1, tq, dqk), lambda b, h, i: (h, b * nq + i, 0)),
                  pl.BlockSpec((1, seq, dqk), lambda b, h, i: (h, b, 0)),
                  pl.BlockSpec((1, seq, dv), lambda b, h, i: (h, b, 0))],
        out_specs=pl.BlockSpec((tq, dv), lambda b, h, i: (b * nq + i, h)),
        out_shape=jax.ShapeDtypeStruct((nb * seq, nh * dv), BF16),
        compiler_params=_cparams(("parallel", "parallel", "arbitrary")),
        name="attn_prompt",
    )(q_hm, k_hm, v_hm)


def _attn_sample_body(q_ref, cckv_ref, ckr_ref, nckv_ref, nkr_ref, wuk_ref, wuv_ref, o_ref, qlat_ref,
                      *, hg, past, scale, all_allowed):
    L = q_ref.shape[1]
    cr = cckv_ref.shape[2]
    for hh in range(hg):
        qlat_ref[hh * L:(hh + 1) * L, :] = _dot_nt(q_ref[hh, :, :A_NOPE], wuk_ref[hh]).astype(BF16)
    qlat = qlat_ref[...]
    qr = q_ref[:, :, A_NOPE:].reshape(hg * L, A_ROPE)
    ckv_p = cckv_ref[0].astype(BF16)
    ckv_n = nckv_ref[...].astype(BF16)
    s_p = (_dot_nt(qlat, ckv_p) + _dot_nt(qr, ckr_ref[0].astype(BF16))) * scale
    s_n = (_dot_nt(qlat, ckv_n) + _dot_nt(qr, nkr_ref[...].astype(BF16))) * scale
    if not all_allowed:
        def masked(s, k0):
            qpos = past + lax.broadcasted_iota(jnp.int32, s.shape, 0) % L
            kpos = k0 + lax.broadcasted_iota(jnp.int32, s.shape, 1)
            return jnp.where((kpos // CHUNK) <= (qpos // CHUNK), s, NEG_INF)

        s_p = masked(s_p, 0)
        s_n = masked(s_n, past)
    m = jnp.maximum(jnp.max(s_p, axis=1, keepdims=True), jnp.max(s_n, axis=1, keepdims=True))
    p_p = jnp.exp(s_p - m)
    p_n = jnp.exp(s_n - m)
    l = jnp.sum(p_p, axis=1, keepdims=True) + jnp.sum(p_n, axis=1, keepdims=True)
    o = (_dot(p_p.astype(BF16), ckv_p) + _dot(p_n.astype(BF16), ckv_n)) / l
    ob = o.astype(BF16)
    for hh in range(hg):
        o_ref[:, hh * A_DV:(hh + 1) * A_DV] = _dot(ob[hh * L:(hh + 1) * L, :], wuv_ref[hh]).astype(o_ref.dtype)


def _attn_sample(q_hm, cache_ckv, cache_kr, ckvn, krope, wuk_h, wuv_h, *, row_blk0, nb, hg=8):
    nh = q_hm.shape[0]
    _, past, cr = cache_ckv.shape
    hg = _tile(nh, hg)
    dqk = A_NOPE + A_ROPE
    all_allowed = (past + CHUNK - 1) // CHUNK <= past // CHUNK
    return pl.pallas_call(
        functools.partial(_attn_sample_body, hg=hg, past=past, scale=dqk ** -0.5, all_allowed=all_allowed),
        grid=(nb, nh // hg),
        in_specs=[pl.BlockSpec((hg, CHUNK, dqk), lambda b, j: (j, row_blk0 + b, 0)),
                  pl.BlockSpec((1, past, cr), lambda b, j: (b, 0, 0)),
                  pl.BlockSpec((1, past, A_ROPE), lambda b, j: (b, 0, 0)),
                  pl.BlockSpec((CHUNK, cr), lambda b, j: (row_blk0 + b, 0)),
                  pl.BlockSpec((CHUNK, A_ROPE), lambda b, j: (row_blk0 + b, 0)),
                  pl.BlockSpec((hg, cr, A_NOPE), lambda b, j: (j, 0, 0)),
                  pl.BlockSpec((hg, cr, A_DV), lambda b, j: (j, 0, 0))],
        out_specs=pl.BlockSpec((CHUNK, hg * A_DV), lambda b, j: (b, j)),
        out_shape=jax.ShapeDtypeStruct((nb * CHUNK, nh * A_DV), BF16),
        scratch_shapes=[pltpu.VMEM((hg * CHUNK, cr), BF16)],
        compiler_params=_cparams(("parallel", "arbitrary")),
        name="attn_sample",
    )(q_hm, cache_ckv, cache_kr, ckvn, krope, wuk_h, wuv_h)


def _extract_topk(x, vals_ref, base):
    n = x.shape[0]
    row = lax.broadcasted_iota(jnp.int32, x.shape, 0)

    def body(it, x):
        m = jnp.max(x, axis=0, keepdims=True)
        vals_ref[pl.ds(base + it, 1), :] = m
        first = jnp.min(jnp.where(x == m, row, n), axis=0, keepdims=True)
        return jnp.where(row == first, NEG_INF, x)

    lax.fori_loop(0, PEER_TOPK, body, x)


def _peer_topk_body(qp_ref, sk_ref, s1_ref, s2_ref, e1_ref, e2_ref, tau_ref, vals_ref, cand_ref, *, nh):
    kk = PEER_TOPK
    nkeys = PEER_NKEYS
    for h in range(nh):
        s1 = _dot_nt(sk_ref[h, 0], qp_ref[:, (2 * h) * nkeys:(2 * h + 1) * nkeys])
        s2 = _dot_nt(sk_ref[h, 1], qp_ref[:, (2 * h + 1) * nkeys:(2 * h + 2) * nkeys])
        s1_ref[h] = s1
        s2_ref[h] = s2
        _extract_topk(s1, vals_ref, 0)
        _extract_topk(s2, vals_ref, kk)
        v2 = vals_ref[kk:2 * kk, :]
        for a in range(kk):
            cand_ref[a * kk:(a + 1) * kk, :] = vals_ref[a:a + 1, :] + v2
        max1 = vals_ref[0:1, :]
        max2 = vals_ref[kk:kk + 1, :]
        _extract_topk(cand_ref[...], vals_ref, 2 * kk)
        best = vals_ref[2 * kk:3 * kk, :]
        zsum = jnp.sum(jnp.exp(best - best[0:1, :]), axis=0, keepdims=True)
        tau_ref[h:h + 1, :] = best[kk - 1:kk, :]
        e1_ref[h] = jnp.exp(s1 - max1)
        e2_ref[h] = jnp.exp(s2 - max2) / zsum


def _peer_topk(qp, subkeys, *, tt=256):
    t = qp.shape[0]
    nh = subkeys.shape[0]
    nkeys = PEER_NKEYS
    tt = _tile(t, tt, 128)
    stat = jax.ShapeDtypeStruct((nh, nkeys, t), F32)
    stat_spec = pl.BlockSpec((nh, nkeys, tt), lambda i: (0, 0, i))
    return pl.pallas_call(
        functools.partial(_peer_topk_body, nh=nh),
        grid=(t // tt,),
        in_specs=[pl.BlockSpec((tt, 2 * nh * nkeys), lambda i: (i, 0)),
                  pl.BlockSpec(subkeys.shape, lambda i: (0, 0, 0, 0))],
        out_specs=[stat_spec, stat_spec, stat_spec, stat_spec, pl.BlockSpec((nh, tt), lambda i: (0, i))],
        out_shape=[stat, stat, stat, stat, jax.ShapeDtypeStruct((nh, t), F32)],
        scratch_shapes=[pltpu.VMEM((3 * PEER_TOPK, tt), F32), pltpu.VMEM((PEER_TOPK * PEER_TOPK, tt), F32)],
        compiler_params=_cparams(("parallel",)),
        name="peer_topk",
    )(qp, subkeys)


def _gelu(x):
    return 0.5 * x * (1.0 + lax.erf(x * (1.0 / math.sqrt(2.0))))


def _peer_dense_body(h_ref, u_ref, v_ref, s1_ref, s2_ref, e1_ref, e2_ref, tau_ref, o_ref, coef_ref, *, nh, ni):
    e = pl.program_id(1)
    nkeys = PEER_NKEYS
    act_t = _dot_nt(u_ref[...], h_ref[...])
    for ii in range(ni):
        i = e * ni + ii
        gate = None
        for h in range(nh):
            s1row = s1_ref[h, pl.ds(i, 1), :]
            e1row = e1_ref[h, pl.ds(i, 1), :]
            hit = (s1row + s2_ref[h]) >= tau_ref[h:h + 1, :]
            term = jnp.where(hit, e1row * e2_ref[h], 0.0)
            gate = term if gate is None else gate + term
        coef_ref[ii * nkeys:(ii + 1) * nkeys, :] = gate * _gelu(act_t[ii * nkeys:(ii + 1) * nkeys, :])
    contrib = _dot(coef_ref[...].T.astype(BF16), v_ref[...])

    @pl.when(e == 0)
    def _():
        o_ref[...] = contrib

    @pl.when(e > 0)
    def _():
        o_ref[...] += contrib


def _peer_dense(h2, u_tab, v_tab, s1, s2, e1, e2, tau, *, tt=512, te=256):
    t, d = h2.shape
    ne = u_tab.shape[0]
    nh = s1.shape[0]
    nkeys = PEER_NKEYS
    tt = _tile(t, tt, 128)
    te = _tile(ne, te, nkeys)
    ni = te // nkeys
    once = pl.Buffered(1)
    stat_spec = pl.BlockSpec((nh, nkeys, tt), lambda i, e: (0, 0, i), pipeline_mode=once)
    return pl.pallas_call(
        functools.partial(_peer_dense_body, nh=nh, ni=ni),
        grid=(t // tt, ne // te),
        in_specs=[pl.BlockSpec((tt, d), lambda i, e: (i, 0), pipeline_mode=once),
                  pl.BlockSpec((te, d), lambda i, e: (e, 0)),
                  pl.BlockSpec((te, d), lambda i, e: (e, 0)),
                  stat_spec, stat_spec, stat_spec, stat_spec,
                  pl.BlockSpec((nh, tt), lambda i, e: (0, i), pipeline_mode=once)],
        out_specs=pl.BlockSpec((tt, d), lambda i, e: (i, 0)),
        out_shape=jax.ShapeDtypeStruct((t, d), F32),
        scratch_shapes=[pltpu.VMEM((te, tt), F32)],
        compiler_params=_cparams(("parallel", "arbitrary"), 56),
        name="peer_dense",
    )(h2, u_tab, v_tab, s1, s2, e1, e2, tau)


def _final_body(x_ref, p_ref, g2_ref, gf_ref, o_ref):
    tr, d = x_ref.shape
    x = x_ref[...].reshape(tr // CHUNK, CHUNK, d) + g2_ref[...] * p_ref[...].reshape(tr // CHUNK, CHUNK, d)
    o_ref[...] = (_rms(x.reshape(tr, d)) * gf_ref[...]).astype(o_ref.dtype)


def _final(x1, pe, modg, gate_idx, gf, *, row_blk0, rows, tr):
    d = x1.shape[1]
    gpt = tr // CHUNK
    r0 = row_blk0 * CHUNK // tr
    return pl.pallas_call(
        _final_body,
        grid=(rows // tr,),
        in_specs=[pl.BlockSpec((tr, d), lambda i: (r0 + i, 0)),
                  pl.BlockSpec((tr, d), lambda i: (r0 + i, 0)),
                  pl.BlockSpec((gpt, 1, d), lambda i: (r0 + i, 0, gate_idx)),
                  pl.BlockSpec((1, d), lambda i: (0, 0))],
        out_specs=pl.BlockSpec((tr, d), lambda i: (i, 0)),
        out_shape=jax.ShapeDtypeStruct((rows, d), F32),
        compiler_params=_cparams(("parallel",)),
        name="final_norm",
    )(x1, pe, modg, gf)


def _rope_tables(pos):
    half = A_ROPE // 2
    inv = ROPE_THETA ** (-jnp.arange(half, dtype=F32) / half)
    ang = pos.astype(F32)[:, None] * inv[None, :]
    cos, sin = jnp.cos(ang), jnp.sin(ang)
    return jnp.concatenate([cos, cos], axis=-1), jnp.concatenate([-sin, sin], axis=-1)


def _rope_q_epilogue(acc, cos, sin_s):
    return jnp.concatenate([acc[:, :A_NOPE], _rope64(acc[:, A_NOPE:], cos, sin_s)], axis=-1)


def _append_krope_epilogue(acc, kr):
    return jnp.concatenate([acc, kr], axis=-1)


def kernel(x_prompt, x_sample, c_prompt, c_sample, cache_ckv, cache_krope, state_C, state_n, state_m, ada_w, ada_b, norm1_g, w_in, b_igate, b_fgate, mlstm_norm_g, q_norm_g, kv_norm_g, w_uq, w_uk, w_uv, w_branch_m, w_branch_a, w_out, norm2_g, peer_wq, peer_subkeys, peer_u, peer_v, final_norm_g):
    nbp, seq, d = x_prompt.shape
    nbs, dseq, _ = x_sample.shape
    depth = ada_w.shape[0]
    assert depth == 1 and seq % CHUNK == 0 and dseq == CHUNK
    past = cache_ckv.shape[2]
    mh, dk, dv = state_C.shape[2:]
    qr, ah, dqk = w_uq.shape[1:]
    cr = w_uk.shape[1]
    assert dqk == A_NOPE + A_ROPE and w_uv.shape[3] == A_DV and ah * A_DV == d and mh * dv == d
    ph = peer_subkeys.shape[1]
    assert peer_subkeys.shape[3] == PEER_NKEYS and peer_subkeys.shape[4] == PEER_NKEYS
    tp, ts = nbp * seq, nbs * dseq
    t = tp + ts
    gp, gs = tp // CHUNK, ts // CHUNK
    ncp = seq // CHUNK

    def tok(want, mult=CHUNK):
        return _tile(math.gcd(tp, ts), want, mult)

    x = jnp.concatenate([x_prompt.reshape(tp, d), x_sample.reshape(ts, d)], axis=0)
    pos = jnp.concatenate([jnp.tile(jnp.arange(seq), nbp), jnp.tile(past + jnp.arange(dseq), nbs)])
    cos_t, sin_t = _rope_tables(pos)

    nc_rows = nbp + nbs
    c_all = jnp.concatenate([c_prompt, c_sample], axis=0)
    c_pad = jnp.pad(c_all, ((0, (-nc_rows) % 16), (0, 0)))
    tn_ada = _tile(6 * d, 512, 128)
    mod = _mm(c_pad, ada_w.reshape(d, 6 * d), out_dtype=F32, tm=c_pad.shape[0], tn=tn_ada, tk=d, a_fn=_silu,
              extras=[(ada_b, (1, tn_ada), lambda i, j: (0, j))], epilogue=lambda acc, b: acc + b,
              name="adaln")
    modg = jnp.concatenate([jnp.repeat(mod[:nbp], ncp, axis=0), mod[nbp:nc_rows]], axis=0)
    modg = modg.reshape(t // CHUNK, 1, 6 * d)

    h1 = _norm_mod(x, norm1_g, modg, 0, 1, tr=tok(256))
    wi = w_in.reshape(d, w_in.shape[2])
    o_mi = 2 * mh * dk + 2 * mh * dv
    o_cq = o_mi + 2 * mh
    o_za = o_cq + qr + cr + A_ROPE
    w_big = jnp.concatenate([wi[:, :o_mi], wi[:, o_za:]], axis=1).astype(BF16)
    ws_cols = qr + cr + A_ROPE + 2 * mh
    ws_pad = (-ws_cols) % 128
    w_small = jnp.concatenate([wi[:, o_cq:o_za], wi[:, o_mi:o_cq], jnp.zeros((d, ws_pad), F32)], axis=1).astype(BF16)
    zb = _mm(h1, w_big, out_dtype=BF16, tm=tok(1024), name="w_in_big")
    zs = _mm(h1, w_small, out_dtype=F32, tm=tok(512), tn=w_small.shape[1], name="w_in_small")
    col_q, col_k, col_v, col_o = 0, mh * dk, 2 * mh * dk, 2 * mh * dk + mh * dv
    col_za = col_o + mh * dv
    col_zb = col_za + d

    gate_bias = jnp.concatenate([b_igate, b_fgate], axis=1)
    cqn, ckvn, krope, gates = _small_post(zs, q_norm_g, kv_norm_g, cos_t, sin_t, gate_bias,
                                          qr=qr, cr=cr, mh=mh, tr=tok(256))

    norm_g3 = mlstm_norm_g.reshape(mh, 1, dv)
    mkw = dict(mh=mh, dk=dk, dv=dv, col_q=col_q, col_k=col_k, col_v=col_v, col_o=col_o)
    hm_p, c_p, n_p, m_p = _mlstm(zb, gates, norm_g3, None, row_blk0=0, nb=nbp, nc=ncp, **mkw)
    st = (state_C[0], state_n[0].reshape(nbs, mh, 1, dk), state_m[0].reshape(nbs, mh, 1, 1))
    hm_s, c_s, n_s, m_s = _mlstm(zb, gates, norm_g3, st, row_blk0=gp, nb=nbs, nc=1, **mkw)
    hm = jnp.concatenate([hm_p, hm_s], axis=0)
    ya = _mm(hm, w_branch_m.reshape(d, d).astype(BF16), out_dtype=F32, tm=tok(1024), name="branch_m")

    wuq_h = jnp.transpose(w_uq[0], (1, 0, 2)).astype(BF16)
    wuk_h = jnp.transpose(w_uk[0], (1, 0, 2)).astype(BF16)
    wuv_h = jnp.transpose(w_uv[0], (1, 0, 2)).astype(BF16)
    tmh = tok(1024)
    q_hm = _mm_heads(cqn, wuq_h, n_out=dqk, out_dtype=BF16, tm=tmh, epilogue=_rope_q_epilogue,
                     extras=[(cos_t, (tmh, A_ROPE), lambda i, h: (i, 0)), (sin_t, (tmh, A_ROPE), lambda i, h: (i, 0))],
                     name="q_up")
    ckvn_b = ckvn[:tp].astype(BF16)
    k_hm = _mm_heads(ckvn_b, wuk_h, n_out=dqk, out_dtype=BF16, tm=tmh, epilogue=_append_krope_epilogue,
                     extras=[(krope, (tmh, A_ROPE), lambda i, h: (i, 0))], name="k_up")
    v_hm = _mm_heads(ckvn_b, wuv_h, n_out=A_DV, out_dtype=BF16, tm=tmh, name="v_up")
    attn_p = _attn_prompt(q_hm, k_hm, v_hm, nb=nbp, seq=seq)
    attn_s = _attn_sample(q_hm, cache_ckv[0], cache_krope[0], ckvn, krope, wuk_h, wuv_h, row_blk0=gp, nb=nbs)
    attn = jnp.concatenate([attn_p, attn_s], axis=0)

    tm, tn = tok(512), _tile(d, 1024, 128)

    def merge(acc, ya_t, za_t, zb_t):
        return jax.nn.sigmoid(za_t.astype(F32)) * ya_t + jax.nn.sigmoid(zb_t.astype(F32)) * acc

    ym = _mm(attn, w_branch_a.reshape(d, d).astype(BF16), out_dtype=BF16, tm=tm, tn=tn, epilogue=merge,
             extras=[(ya, (tm, tn), lambda i, j: (i, j)),
                     (zb, (tm, tn), lambda i, j: (i, col_za // tn + j)),
                     (zb, (tm, tn), lambda i, j: (i, col_zb // tn + j))], name="branch_a_merge")

    def residual(acc, x_t, g_t):
        r, cdim = acc.shape
        y = x_t.reshape(r // CHUNK, CHUNK, cdim) + g_t * acc.reshape(r // CHUNK, CHUNK, cdim)
        return y.reshape(r, cdim)

    x1 = _mm(ym, w_out.reshape(d, d).astype(BF16), out_dtype=F32, tm=tm, tn=tn, epilogue=residual,
             extras=[(x, (tm, tn), lambda i, j: (i, j)),
                     (modg, (tm // CHUNK, 1, tn), lambda i, j: (i, 0, 2 * (d // tn) + j))], name="w_out")

    h2 = _norm_mod(x1, norm2_g, modg, 3, 4, tr=tok(256))
    qp = _mm(h2, peer_wq.reshape(d, -1).astype(BF16), out_dtype=BF16, tm=tok(1024), name="peer_q")
    s1, s2, e1, e2, tau = _peer_topk(qp, peer_subkeys[0].astype(BF16), tt=tok(256, 128))
    ne = peer_u.shape[1]
    pe = _peer_dense(h2, peer_u.reshape(ne, d).astype(BF16), peer_v.reshape(ne, d).astype(BF16),
                     s1, s2, e1, e2, tau, tt=tok(512, 128))

    gf = final_norm_g.reshape(1, d)
    y_p = _final(x1, pe, modg, 5, gf, row_blk0=0, rows=tp, tr=tok(256)).reshape(nbp, seq, d)
    y_s = _final(x1, pe, modg, 5, gf, row_blk0=gp, rows=ts, tr=tok(256)).reshape(nbs, dseq, d)

    return (y_p, y_s,
            ckvn[:tp].reshape(1, nbp, seq, cr), krope[:tp].reshape(1, nbp, seq, A_ROPE),
            c_p[None], n_p.reshape(1, nbp, mh, dk), m_p.reshape(1, nbp, mh),
            ckvn[tp:].reshape(1, nbs, dseq, cr), krope[tp:].reshape(1, nbs, dseq, A_ROPE),
            c_s[None], n_s.reshape(1, nbs, mh, dk), m_s.reshape(1, nbs, mh))
```

```python
import functools
import math

import jax
import jax.numpy as jnp
from jax import lax
from jax.experimental import pallas as pl
from jax.experimental.pallas import tpu as pltpu

CHUNK = 64
A_NOPE = 128
A_ROPE = 64
A_DV = 128
ROPE_THETA = 10000.0
PEER_NKEYS = 128
PEER_TOPK = 16
NORM_EPS = 1e-6

F32 = jnp.float32
BF16 = jnp.bfloat16
MIB = 1024 * 1024
NEG_INF = float("-inf")
SUBLANES = 8


def _cparams(sem, vmem_mib=48):
    return pltpu.CompilerParams(dimension_semantics=sem, vmem_limit_bytes=vmem_mib * MIB)


def _tile(n, want, mult=1):
    if n <= want:
        return n
    t = (want // mult) * mult
    while t >= mult:
        if n % t == 0:
            return t
        t -= mult
    return n


def _dot(a, b):
    return jnp.dot(a, b, preferred_element_type=F32)


def _dot_nt(a, b):
    return lax.dot_general(a, b, (((1,), (1,)), ((), ())), preferred_element_type=F32)


def _silu(x):
    return x * jax.nn.sigmoid(x)


def _mm_body(*refs, nk, n_extra, a_fn, epilogue):
    a_ref, w_ref = refs[0], refs[1]
    extras = refs[2:2 + n_extra]
    o_ref = refs[2 + n_extra]
    a = a_ref[...]
    if a_fn is not None:
        a = a_fn(a)
    part = _dot(a.astype(BF16), w_ref[...].astype(BF16))

    def finish(acc):
        res = acc if epilogue is None else epilogue(acc, *[e[...] for e in extras])
        o_ref[...] = res.astype(o_ref.dtype)

    if nk == 1:
        finish(part)
        return
    acc_ref = refs[3 + n_extra]
    k = pl.program_id(2)

    @pl.when(k == 0)
    def _():
        acc_ref[...] = part

    @pl.when(k > 0)
    def _():
        acc_ref[...] += part

    @pl.when(k == nk - 1)
    def _():
        finish(acc_ref[...])


def _mm(a, w, *, out_dtype, tm=1024, tn=1024, tk=2048, extras=(), epilogue=None, a_fn=None,
        vmem_mib=48, name="mm"):
    m, kdim = a.shape
    n_out = w.shape[1]
    tm = _tile(m, tm, 8)
    tn = _tile(n_out, tn, 128)
    tk = _tile(kdim, tk, 128)
    nk = kdim // tk
    grid = (m // tm, n_out // tn, nk)
    in_specs = [pl.BlockSpec((tm, tk), lambda i, j, k: (i, k)),
                pl.BlockSpec((tk, tn), lambda i, j, k: (k, j))]
    args = [a, w]
    for arr, bshape, imap in extras:
        in_specs.append(pl.BlockSpec(bshape, functools.partial(lambda i, j, k, f: f(i, j), f=imap)))
        args.append(arr)
    scratch = [pltpu.VMEM((tm, tn), F32)] if nk > 1 else []
    return pl.pallas_call(
        functools.partial(_mm_body, nk=nk, n_extra=len(extras), a_fn=a_fn, epilogue=epilogue),
        grid=grid,
        in_specs=in_specs,
        out_specs=pl.BlockSpec((tm, tn), lambda i, j, k: (i, j)),
        out_shape=jax.ShapeDtypeStruct((m, n_out), out_dtype),
        scratch_shapes=scratch,
        compiler_params=_cparams(("parallel", "parallel", "arbitrary"), vmem_mib),
        name=name,
    )(*args)


def _mmh_body(*refs, n_extra, epilogue):
    a_ref, w_ref = refs[0], refs[1]
    extras = refs[2:2 + n_extra]
    o_ref = refs[2 + n_extra]
    acc = _dot(a_ref[...], w_ref[0])
    res = acc if epilogue is None else epilogue(acc, *[e[...] for e in extras])
    o_ref[0] = res.astype(o_ref.dtype)


def _mm_heads(a, w, *, n_out, out_dtype, tm=1024, extras=(), epilogue=None, name="mm_heads"):
    m, kdim = a.shape
    nh, _, n = w.shape
    tm = _tile(m, tm, 8)
    in_specs = [pl.BlockSpec((tm, kdim), lambda i, h: (i, 0)),
                pl.BlockSpec((1, kdim, n), lambda i, h: (h, 0, 0))]
    args = [a, w]
    for arr, bshape, imap in extras:
        in_specs.append(pl.BlockSpec(bshape, imap))
        args.append(arr)
    return pl.pallas_call(
        functools.partial(_mmh_body, n_extra=len(extras), epilogue=epilogue),
        grid=(m // tm, nh),
        in_specs=in_specs,
        out_specs=pl.BlockSpec((1, tm, n_out), lambda i, h: (h, i, 0)),
        out_shape=jax.ShapeDtypeStruct((nh, m, n_out), out_dtype),
        compiler_params=_cparams(("parallel", "arbitrary")),
        name=name,
    )(*args)


def _rms(x):
    return x * lax.rsqrt(jnp.mean(x * x, axis=-1, keepdims=True) + NORM_EPS)


def _norm_mod_body(x_ref, g_ref, sh_ref, sc_ref, o_ref, *ot_ref):
    tr, d = x_ref.shape
    y = _rms(x_ref[...]) * g_ref[...]
    y = (y.reshape(tr // CHUNK, CHUNK, d) * (1.0 + sc_ref[...]) + sh_ref[...]).reshape(tr, d)
    o_ref[...] = y.astype(o_ref.dtype)
    if ot_ref:
        ot_ref[0][...] = y.T.astype(o_ref.dtype)


def _norm_mod(x, g, modg, shift_idx, scale_idx, *, tr, with_transposed=False):
    t, d = x.shape
    gpt = tr // CHUNK
    out_specs = [pl.BlockSpec((tr, d), lambda i: (i, 0))]
    out_shape = [jax.ShapeDtypeStruct((t, d), BF16)]
    if with_transposed:
        out_specs.append(pl.BlockSpec((d, tr), lambda i: (0, i)))
        out_shape.append(jax.ShapeDtypeStruct((d, t), BF16))
    res = pl.pallas_call(
        _norm_mod_body,
        grid=(t // tr,),
        in_specs=[pl.BlockSpec((tr, d), lambda i: (i, 0)),
                  pl.BlockSpec((1, d), lambda i: (0, 0)),
                  pl.BlockSpec((gpt, 1, d), lambda i: (i, 0, shift_idx)),
                  pl.BlockSpec((gpt, 1, d), lambda i: (i, 0, scale_idx))],
        out_specs=out_specs,
        out_shape=out_shape,
        compiler_params=_cparams(("parallel",)),
        name="norm_mod",
    )(x, g, modg, modg)
    return res if with_transposed else res[0]


def _rope64(x, cos, sin_signed):
    half = A_ROPE // 2
    rolled = jnp.concatenate([x[:, half:], x[:, :half]], axis=-1)
    return x * cos + rolled * sin_signed


def _log_sigmoid(x):
    return jnp.minimum(x, 0.0) - jnp.log1p(jnp.exp(-jnp.abs(x)))


def _small_post_body(z_ref, qg_ref, kvg_ref, cos_ref, sin_ref, bias_ref,
                     cq_ref, ckv_ref, kr_ref, gate_ref, *, qr, cr, mh):
    z = z_ref[...]
    cq_ref[...] = (_rms(z[:, :qr]) * qg_ref[...]).astype(cq_ref.dtype)
    ckv_ref[...] = _rms(z[:, qr:qr + cr]) * kvg_ref[...]
    o = qr + cr
    kr_ref[...] = _rope64(z[:, o:o + A_ROPE], cos_ref[...], sin_ref[...])
    o += A_ROPE
    pre = z[:, o:o + 2 * mh] + bias_ref[...]
    lane = lax.broadcasted_iota(jnp.int32, pre.shape, 1)
    gate_ref[...] = jnp.where(lane < mh, pre, _log_sigmoid(pre))


def _small_post(zs, qg, kvg, cos, sin_s, gate_bias, *, qr, cr, mh, tr):
    t, w = zs.shape
    row = lambda i: (i, 0)
    fix = lambda i: (0, 0)
    return pl.pallas_call(
        functools.partial(_small_post_body, qr=qr, cr=cr, mh=mh),
        grid=(t // tr,),
        in_specs=[pl.BlockSpec((tr, w), row), pl.BlockSpec((1, qr), fix), pl.BlockSpec((1, cr), fix),
                  pl.BlockSpec((tr, A_ROPE), row), pl.BlockSpec((tr, A_ROPE), row),
                  pl.BlockSpec((1, 2 * mh), fix)],
        out_specs=[pl.BlockSpec((tr, qr), row), pl.BlockSpec((tr, cr), row),
                   pl.BlockSpec((tr, A_ROPE), row), pl.BlockSpec((tr, 2 * mh), row)],
        out_shape=[jax.ShapeDtypeStruct((t, qr), BF16), jax.ShapeDtypeStruct((t, cr), F32),
                   jax.ShapeDtypeStruct((t, A_ROPE), F32), jax.ShapeDtypeStruct((t, 2 * mh), F32)],
        compiler_params=_cparams(("parallel",)),
        name="small_post",
    )(zs, qg, kvg, cos, sin_s, gate_bias)


def _mlstm_head(q, k, v, mo, ig_col, lf_col, g, c_prev, n_prev, m_prev, scale):
    L = q.shape[0]
    r = lax.broadcasted_iota(jnp.int32, (L, L), 0)
    c = lax.broadcasted_iota(jnp.int32, (L, L), 1)
    eye = r == c
    ig_row = jnp.sum(jnp.where(eye, ig_col, 0.0), axis=0, keepdims=True)
    lf_row = jnp.sum(jnp.where(eye, lf_col, 0.0), axis=0, keepdims=True)
    causal = c <= r
    b_col = jnp.sum(jnp.where(causal, lf_row, 0.0), axis=1, keepdims=True)
    b_row = jnp.sum(jnp.where(r <= c, lf_col, 0.0), axis=0, keepdims=True)
    b_last = jnp.sum(lf_col, axis=0, keepdims=True)

    a_col = b_col + m_prev
    dmat = jnp.where(causal, b_col - b_row + ig_row, NEG_INF)
    m_t = jnp.maximum(a_col, jnp.max(dmat, axis=1, keepdims=True))
    w_inter = jnp.exp(a_col - m_t)
    w_intra = jnp.exp(dmat - m_t)

    qs = q.astype(F32) * scale
    qb = qs.astype(BF16)
    s = _dot_nt(qb, k) * w_intra
    num = _dot(s.astype(BF16), v) + w_inter * _dot(qb, c_prev.astype(BF16))
    den = jnp.sum(s, axis=1, keepdims=True) + w_inter * jnp.sum(qs * n_prev, axis=1, keepdims=True)
    h = num / jnp.maximum(jnp.abs(den), jnp.exp(-m_t))
    h_out = _rms(h) * g * jax.nn.sigmoid(mo.astype(F32))

    m_new = jnp.sum(jnp.where(lax.broadcasted_iota(jnp.int32, (L, 1), 0) == L - 1, m_t, 0.0),
                    axis=0, keepdims=True)
    w_c = jnp.exp(b_last + m_prev - m_new)
    w_s = jnp.exp(b_last - b_col + ig_col - m_new)
    kw = k.astype(F32) * w_s
    c_new = w_c * c_prev + _dot(kw.T.astype(BF16), v)
    n_new = w_c * n_prev + jnp.sum(kw, axis=0, keepdims=True)
    return h_out, c_new, n_new, m_new


def _mlstm_body(*refs, has_state, mh, dk, dv, scale):
    if has_state:
        (q_ref, k_ref, v_ref, mo_ref, gate_ref, g_ref, c0_ref, n0_ref, m0_ref,
         h_ref, c_ref, n_ref, m_ref) = refs
    else:
        (q_ref, k_ref, v_ref, mo_ref, gate_ref, g_ref, h_ref, c_ref, n_ref, m_ref) = refs
    ci = pl.program_id(1)

    @pl.when(ci == 0)
    def _():
        if has_state:
            c_ref[...] = c0_ref[...]
            n_ref[...] = n0_ref[...]
            m_ref[...] = m0_ref[...]
        else:
            c_ref[...] = jnp.zeros_like(c_ref)
            n_ref[...] = jnp.zeros_like(n_ref)
            m_ref[...] = jnp.zeros_like(m_ref)

    gates = gate_ref[...]
    for hd in range(mh):
        h_out, c_new, n_new, m_new = _mlstm_head(
            q_ref[:, hd * dk:(hd + 1) * dk], k_ref[:, hd * dk:(hd + 1) * dk],
            v_ref[:, hd * dv:(hd + 1) * dv], mo_ref[:, hd * dv:(hd + 1) * dv],
            gates[:, hd:hd + 1], gates[:, mh + hd:mh + hd + 1], g_ref[:, hd * dv:(hd + 1) * dv],
            c_ref[0, hd], n_ref[0, hd], m_ref[0, hd], scale)
        h_ref[:, hd * dv:(hd + 1) * dv] = h_out.astype(h_ref.dtype)
        c_ref[0, hd] = c_new
        n_ref[0, hd] = n_new
        m_ref[0, hd] = m_new


def _mlstm(zb, gates, norm_g, state, *, row_blk0, nb, nc, mh, dk, dv, blk_q, blk_k, blk_v, blk_o):
    has_state = state is not None
    wq, wv = mh * dk, mh * dv
    rowmap = lambda blk: (lambda b, c: (row_blk0 + b * nc + c, blk))
    in_specs = [pl.BlockSpec((CHUNK, wq), rowmap(blk_q)),
                pl.BlockSpec((CHUNK, wq), rowmap(blk_k)),
                pl.BlockSpec((CHUNK, wv), rowmap(blk_v)),
                pl.BlockSpec((CHUNK, wv), rowmap(blk_o)),
                pl.BlockSpec((CHUNK, 2 * mh), rowmap(0)),
                pl.BlockSpec((1, wv), lambda b, c: (0, 0))]
    args = [zb, zb, zb, zb, gates, norm_g]
    st_specs = [pl.BlockSpec((1, mh, dk, dv), lambda b, c: (b, 0, 0, 0)),
                pl.BlockSpec((1, mh, 1, dk), lambda b, c: (b, 0, 0, 0)),
                pl.BlockSpec((1, mh, 1, 1), lambda b, c: (b, 0, 0, 0))]
    if has_state:
        in_specs += st_specs
        args += list(state)
    return pl.pallas_call(
        functools.partial(_mlstm_body, has_state=has_state, mh=mh, dk=dk, dv=dv, scale=dk ** -0.5),
        grid=(nb, nc),
        in_specs=in_specs,
        out_specs=[pl.BlockSpec((CHUNK, wv), lambda b, c: (b * nc + c, 0))] + st_specs,
        out_shape=[jax.ShapeDtypeStruct((nb * nc * CHUNK, wv), BF16),
                   jax.ShapeDtypeStruct((nb, mh, dk, dv), F32),
                   jax.ShapeDtypeStruct((nb, mh, 1, dk), F32),
                   jax.ShapeDtypeStruct((nb, mh, 1, 1), F32)],
        compiler_params=_cparams(("parallel", "arbitrary")),
        name="mlstm",
    )(*args)


def _attn_prompt_body(q_ref, ckv_ref, kr_ref, wuk_ref, wuv_ref, o_ref, k_scr, v_scr, *, tq, scale):
    qi = pl.program_id(2)

    @pl.when(qi == 0)
    def _():
        ckv = ckv_ref[...].astype(BF16)
        k_scr[:, :A_NOPE] = _dot(ckv, wuk_ref[0]).astype(BF16)
        k_scr[:, A_NOPE:] = kr_ref[...].astype(BF16)
        v_scr[...] = _dot(ckv, wuv_ref[0]).astype(BF16)

    q = q_ref[0]

    def step(kb, carry, diagonal):
        m, l, acc = carry
        rows = pl.ds(pl.multiple_of(kb * tq, tq), tq)
        s = _dot_nt(q, k_scr[rows, :]) * scale
        if diagonal:
            qc = lax.broadcasted_iota(jnp.int32, (tq, tq), 0) // CHUNK
            kc = lax.broadcasted_iota(jnp.int32, (tq, tq), 1) // CHUNK
            s = jnp.where(kc <= qc, s, NEG_INF)
        m_new = jnp.maximum(m, jnp.max(s, axis=1, keepdims=True))
        alpha = jnp.exp(m - m_new)
        p = jnp.exp(s - m_new)
        l = alpha * l + jnp.sum(p, axis=1, keepdims=True)
        acc = alpha * acc + _dot(p.astype(BF16), v_scr[rows, :])
        return m_new, l, acc

    init = (jnp.full((tq, 1), NEG_INF, F32), jnp.zeros((tq, 1), F32), jnp.zeros((tq, A_DV), F32))
    carry = lax.fori_loop(0, qi, lambda kb, c: step(kb, c, False), init)
    m, l, acc = step(qi, carry, True)
    o_ref[...] = (acc / l).astype(o_ref.dtype)


def _attn_prompt(q_hm, ckvn, krope, wuk_h, wuv_h, *, nb, seq, tq=512):
    nh, _, dqk = q_hm.shape
    cr = ckvn.shape[1]
    tq = _tile(seq, tq, CHUNK)
    nq = seq // tq
    return pl.pallas_call(
        functools.partial(_attn_prompt_body, tq=tq, scale=dqk ** -0.5),
        grid=(nb, nh, nq),
        in_specs=[pl.BlockSpec((1, tq, dqk), lambda b, h, i: (h, b * nq + i, 0)),
                  pl.BlockSpec((seq, cr), lambda b, h, i: (b, 0)),
                  pl.BlockSpec((seq, A_ROPE), lambda b, h, i: (b, 0)),
                  pl.BlockSpec((1, cr, A_NOPE), lambda b, h, i: (h, 0, 0)),
                  pl.BlockSpec((1, cr, A_DV), lambda b, h, i: (h, 0, 0))],
        out_specs=pl.BlockSpec((tq, A_DV), lambda b, h, i: (b * nq + i, h)),
        out_shape=jax.ShapeDtypeStruct((nb * seq, nh * A_DV), BF16),
        scratch_shapes=[pltpu.VMEM((seq, dqk), BF16), pltpu.VMEM((seq, A_DV), BF16)],
        compiler_params=_cparams(("parallel", "arbitrary", "arbitrary")),
        name="attn_prompt",
    )(q_hm, ckvn, krope, wuk_h, wuv_h)


def _attn_sample_body(q_ref, cckv_ref, ckr_ref, nckv_ref, nkr_ref, wuk_ref, wuv_ref, o_ref, qlat_ref,
                      *, hg, past, scale, all_allowed):
    L = q_ref.shape[1]
    for hh in range(hg):
        qlat_ref[hh * L:(hh + 1) * L, :] = _dot_nt(q_ref[hh, :, :A_NOPE], wuk_ref[hh]).astype(BF16)
    qlat = qlat_ref[...]
    qr = q_ref[:, :, A_NOPE:].reshape(hg * L, A_ROPE)
    ckv_p = cckv_ref[0].astype(BF16)
    ckv_n = nckv_ref[...].astype(BF16)
    s_p = (_dot_nt(qlat, ckv_p) + _dot_nt(qr, ckr_ref[0].astype(BF16))) * scale
    s_n = (_dot_nt(qlat, ckv_n) + _dot_nt(qr, nkr_ref[...].astype(BF16))) * scale
    if not all_allowed:
        def masked(s, k0):
            qpos = past + lax.broadcasted_iota(jnp.int32, s.shape, 0) % L
            kpos = k0 + lax.broadcasted_iota(jnp.int32, s.shape, 1)
            return jnp.where((kpos // CHUNK) <= (qpos // CHUNK), s, NEG_INF)

        s_p = masked(s_p, 0)
        s_n = masked(s_n, past)
    m = jnp.maximum(jnp.max(s_p, axis=1, keepdims=True), jnp.max(s_n, axis=1, keepdims=True))
    p_p = jnp.exp(s_p - m)
    p_n = jnp.exp(s_n - m)
    l = jnp.sum(p_p, axis=1, keepdims=True) + jnp.sum(p_n, axis=1, keepdims=True)
    o = (_dot(p_p.astype(BF16), ckv_p) + _dot(p_n.astype(BF16), ckv_n)) / l
    ob = o.astype(BF16)
    for hh in range(hg):
        o_ref[:, hh * A_DV:(hh + 1) * A_DV] = _dot(ob[hh * L:(hh + 1) * L, :], wuv_ref[hh]).astype(o_ref.dtype)


def _attn_sample(q_hm, cache_ckv, cache_kr, ckvn, krope, wuk_h, wuv_h, *, row_blk0, nb, hg=8):
    nh = q_hm.shape[0]
    _, past, cr = cache_ckv.shape
    hg = _tile(nh, hg)
    dqk = A_NOPE + A_ROPE
    all_allowed = (past + CHUNK - 1) // CHUNK <= past // CHUNK
    return pl.pallas_call(
        functools.partial(_attn_sample_body, hg=hg, past=past, scale=dqk ** -0.5, all_allowed=all_allowed),
        grid=(nb, nh // hg),
        in_specs=[pl.BlockSpec((hg, CHUNK, dqk), lambda b, j: (j, row_blk0 + b, 0)),
                  pl.BlockSpec((1, past, cr), lambda b, j: (b, 0, 0)),
                  pl.BlockSpec((1, past, A_ROPE), lambda b, j: (b, 0, 0)),
                  pl.BlockSpec((CHUNK, cr), lambda b, j: (row_blk0 + b, 0)),
                  pl.BlockSpec((CHUNK, A_ROPE), lambda b, j: (row_blk0 + b, 0)),
                  pl.BlockSpec((hg, cr, A_NOPE), lambda b, j: (j, 0, 0)),
                  pl.BlockSpec((hg, cr, A_DV), lambda b, j: (j, 0, 0))],
        out_specs=pl.BlockSpec((CHUNK, hg * A_DV), lambda b, j: (b, j)),
        out_shape=jax.ShapeDtypeStruct((nb * CHUNK, nh * A_DV), BF16),
        scratch_shapes=[pltpu.VMEM((hg * CHUNK, cr), BF16)],
        compiler_params=_cparams(("parallel", "arbitrary")),
        name="attn_sample",
    )(q_hm, cache_ckv, cache_kr, ckvn, krope, wuk_h, wuv_h)


def _extract_topk(x, vals_ref, base):
    n = x.shape[0]
    row = lax.broadcasted_iota(jnp.int32, x.shape, 0)

    def body(it, x):
        m = jnp.max(x, axis=0, keepdims=True)
        vals_ref[pl.ds(base + it, 1), :] = m
        first = jnp.min(jnp.where(x == m, row, n), axis=0, keepdims=True)
        return jnp.where(row == first, NEG_INF, x)

    lax.fori_loop(0, PEER_TOPK, body, x)


def _cand_layout():
    kk = PEER_TOPK
    groups, a = [], 0
    while kk // (a + 1) > 1:
        nv = kk // (a + 1)
        groups.append((a, nv, -(-nv // SUBLANES) * SUBLANES))
        a += 1
    return groups, a


def _peer_topk_body(qp_ref, sk_ref, s1_ref, s2_ref, e1_ref, e2_ref, tau_ref, vals_ref, cand_ref, *, nh):
    kk = PEER_TOPK
    nkeys = PEER_NKEYS
    groups, tail = _cand_layout()
    for h in range(nh):
        s1 = _dot_nt(sk_ref[h, 0], qp_ref[:, (2 * h) * nkeys:(2 * h + 1) * nkeys])
        s2 = _dot_nt(sk_ref[h, 1], qp_ref[:, (2 * h + 1) * nkeys:(2 * h + 2) * nkeys])
        s1_ref[h] = s1
        s2_ref[h] = s2
        _extract_topk(s1, vals_ref, 0)
        _extract_topk(s2, vals_ref, kk)
        r0 = 0
        for a, nv, nr in groups:
            blk = vals_ref[a:a + 1, :] + vals_ref[kk:kk + nr, :]
            if nv < nr:
                blk = jnp.where(lax.broadcasted_iota(jnp.int32, blk.shape, 0) < nv, blk, NEG_INF)
            cand_ref[r0:r0 + nr, :] = blk
            r0 += nr
        cand_ref[r0:r0 + kk - tail, :] = vals_ref[tail:kk, :] + vals_ref[kk:kk + 1, :]
        max1 = vals_ref[0:1, :]
        max2 = vals_ref[kk:kk + 1, :]
        _extract_topk(cand_ref[...], vals_ref, 2 * kk)
        best = vals_ref[2 * kk:3 * kk, :]
        zsum = jnp.sum(jnp.exp(best - best[0:1, :]), axis=0, keepdims=True)
        tau_ref[h:h + 1, :] = best[kk - 1:kk, :]
        e1_ref[h] = jnp.exp(s1 - max1)
        e2_ref[h] = jnp.exp(s2 - max2) / zsum


def _peer_topk(qp, subkeys, *, tt):
    t = qp.shape[0]
    nh = subkeys.shape[0]
    nkeys = PEER_NKEYS
    groups, tail = _cand_layout()
    n_cand = sum(nr for _, _, nr in groups) + PEER_TOPK - tail
    stat = jax.ShapeDtypeStruct((nh, nkeys, t), F32)
    stat_spec = pl.BlockSpec((nh, nkeys, tt), lambda i: (0, 0, i))
    return pl.pallas_call(
        functools.partial(_peer_topk_body, nh=nh),
        grid=(t // tt,),
        in_specs=[pl.BlockSpec((tt, 2 * nh * nkeys), lambda i: (i, 0)),
                  pl.BlockSpec(subkeys.shape, lambda i: (0, 0, 0, 0))],
        out_specs=[stat_spec, stat_spec, stat_spec, stat_spec, pl.BlockSpec((nh, tt), lambda i: (0, i))],
        out_shape=[stat, stat, stat, stat, jax.ShapeDtypeStruct((nh, t), F32)],
        scratch_shapes=[pltpu.VMEM((4 * PEER_TOPK, tt), F32), pltpu.VMEM((n_cand, tt), F32)],
        compiler_params=_cparams(("parallel",)),
        name="peer_topk",
    )(qp, subkeys)


def _gelu(x):
    return 0.5 * x * (1.0 + lax.erf(x * (1.0 / math.sqrt(2.0))))


def _peer_dense_body(ht_ref, u_ref, vt_ref, s1_ref, s2_ref, e1_ref, e2_ref, tau_ref, o_ref,
                     act_a, act_b, coef_a, coef_b, *, nh, ni, n_et, dchunk):
    e = pl.program_id(1)
    nkeys = PEER_NKEYS
    d, tt = o_ref.shape
    lanes = 128

    @pl.when(e == 0)
    def _():
        o_ref[...] = jnp.zeros_like(o_ref)
        act_b[...] = jnp.zeros_like(act_b)
        coef_b[...] = jnp.zeros_like(coef_b)

    tile_c = jnp.clip(e - 1, 0, n_et - 1)

    def step(act_w, act_r, coef_w, coef_r):
        def activations():
            act_w[...] = _dot(u_ref[...], ht_ref[...])

        def coefficients(ii, c0):
            i = tile_c * ni + ii
            rows, cols = slice(ii * nkeys, (ii + 1) * nkeys), slice(c0, c0 + lanes)
            gate = None
            for h in range(nh):
                s1row = s1_ref[h, pl.ds(i, 1), :][:, cols]
                e1row = e1_ref[h, pl.ds(i, 1), :][:, cols]
                hit = (s1row + s2_ref[h, :, cols]) >= tau_ref[h:h + 1, cols]
                term = jnp.where(hit, e1row * e2_ref[h, :, cols], 0.0)
                gate = term if gate is None else gate + term
            coef_w[rows, cols] = (gate * _gelu(act_r[rows, cols])).astype(coef_w.dtype)

        def drain(r0):
            o_ref[r0:r0 + dchunk, :] += _dot(vt_ref[r0:r0 + dchunk, :], coef_r[...])

        mxu_items = [functools.partial(drain, r0) for r0 in range(0, d, dchunk)]
        mxu_items.insert(len(mxu_items) // 2, activations)
        vpu_items = [functools.partial(coefficients, ii, c0) for ii in range(ni) for c0 in range(0, tt, lanes)]
        n_m, n_v = len(mxu_items), len(vpu_items)
        done_v = 0
        for k, item in enumerate(mxu_items):
            item()
            upto = (k + 1) * n_v // n_m
            for v in vpu_items[done_v:upto]:
                v()
            done_v = upto

    @pl.when(e % 2 == 0)
    def _():
        step(act_a, act_b, coef_a, coef_b)

    @pl.when(e % 2 == 1)
    def _():
        step(act_b, act_a, coef_b, coef_a)


def _peer_dense(h2_t, u_tab, vt_tab, s1, s2, e1, e2, tau, *, tt, te=512, dchunk=512):
    d, t = h2_t.shape
    ne = u_tab.shape[0]
    nh = s1.shape[0]
    nkeys = PEER_NKEYS
    te = _tile(ne, te, nkeys)
    ni = te // nkeys
    n_et = ne // te
    dchunk = _tile(d, dchunk, 8)
    once = pl.Buffered(1)
    stat_spec = pl.BlockSpec((nh, nkeys, tt), lambda i, e: (0, 0, i), pipeline_mode=once)
    return pl.pallas_call(
        functools.partial(_peer_dense_body, nh=nh, ni=ni, n_et=n_et, dchunk=dchunk),
        grid=(t // tt, n_et + 2),
        in_specs=[pl.BlockSpec((d, tt), lambda i, e: (0, i), pipeline_mode=once),
                  pl.BlockSpec((te, d), lambda i, e: (jnp.minimum(e, n_et - 1), 0)),
                  pl.BlockSpec((d, te), lambda i, e: (0, jnp.clip(e - 2, 0, n_et - 1))),
                  stat_spec, stat_spec, stat_spec, stat_spec,
                  pl.BlockSpec((nh, tt), lambda i, e: (0, i), pipeline_mode=once)],
        out_specs=pl.BlockSpec((d, tt), lambda i, e: (0, i)),
        out_shape=jax.ShapeDtypeStruct((d, t), F32),
        scratch_shapes=[pltpu.VMEM((te, tt), F32), pltpu.VMEM((te, tt), F32),
                        pltpu.VMEM((te, tt), BF16), pltpu.VMEM((te, tt), BF16)],
        compiler_params=_cparams(("parallel", "arbitrary"), 56),
        name="peer_dense",
    )(h2_t, u_tab, vt_tab, s1, s2, e1, e2, tau)


def _final_body(x_ref, pt_ref, g2_ref, gf_ref, o_ref):
    tr, d = x_ref.shape
    p = pt_ref[...].T
    x = x_ref[...].reshape(tr // CHUNK, CHUNK, d) + g2_ref[...] * p.reshape(tr // CHUNK, CHUNK, d)
    o_ref[...] = (_rms(x.reshape(tr, d)) * gf_ref[...]).astype(o_ref.dtype)


def _final(x1, pe_t, modg, gate_idx, gf, *, row_blk0, rows, tr):
    d = x1.shape[1]
    gpt = tr // CHUNK
    r0 = row_blk0 * CHUNK // tr
    return pl.pallas_call(
        _final_body,
        grid=(rows // tr,),
        in_specs=[pl.BlockSpec((tr, d), lambda i: (r0 + i, 0)),
                  pl.BlockSpec((d, tr), lambda i: (0, r0 + i)),
                  pl.BlockSpec((gpt, 1, d), lambda i: (r0 + i, 0, gate_idx)),
                  pl.BlockSpec((1, d), lambda i: (0, 0))],
        out_specs=pl.BlockSpec((tr, d), lambda i: (i, 0)),
        out_shape=jax.ShapeDtypeStruct((rows, d), F32),
        compiler_params=_cparams(("parallel",)),
        name="final_norm",
    )(x1, pe_t, modg, gf)


def _rope_tables(pos):
    half = A_ROPE // 2
    inv = ROPE_THETA ** (-jnp.arange(half, dtype=F32) / half)
    ang = pos.astype(F32)[:, None] * inv[None, :]
    cos, sin = jnp.cos(ang), jnp.sin(ang)
    return jnp.concatenate([cos, cos], axis=-1), jnp.concatenate([-sin, sin], axis=-1)


def _rope_q_epilogue(acc, cos, sin_s):
    return jnp.concatenate([acc[:, :A_NOPE], _rope64(acc[:, A_NOPE:], cos, sin_s)], axis=-1)


def kernel(x_prompt, x_sample, c_prompt, c_sample, cache_ckv, cache_krope, state_C, state_n, state_m, ada_w, ada_b, norm1_g, w_in, b_igate, b_fgate, mlstm_norm_g, q_norm_g, kv_norm_g, w_uq, w_uk, w_uv, w_branch_m, w_branch_a, w_out, norm2_g, peer_wq, peer_subkeys, peer_u, peer_v, final_norm_g):
    nbp, seq, d = x_prompt.shape
    nbs, dseq, _ = x_sample.shape
    depth = ada_w.shape[0]
    assert depth == 1 and seq % CHUNK == 0 and dseq == CHUNK
    past = cache_ckv.shape[2]
    mh, dk, dv = state_C.shape[2:]
    qr, ah, dqk = w_uq.shape[1:]
    cr = w_uk.shape[1]
    assert dqk == A_NOPE + A_ROPE and w_uv.shape[3] == A_DV and ah * A_DV == d and mh * dv == d
    assert (2 * dk) % dv == 0
    assert peer_subkeys.shape[3] == PEER_NKEYS and peer_subkeys.shape[4] == PEER_NKEYS
    tp, ts = nbp * seq, nbs * dseq
    t = tp + ts
    gp = tp // CHUNK
    ncp = seq // CHUNK

    def tok(want, mult=CHUNK):
        return _tile(math.gcd(tp, ts), want, mult)

    x = jnp.concatenate([x_prompt.reshape(tp, d), x_sample.reshape(ts, d)], axis=0)
    pos = jnp.concatenate([jnp.tile(jnp.arange(seq), nbp), jnp.tile(past + jnp.arange(dseq), nbs)])
    cos_t, sin_t = _rope_tables(pos)

    nc_rows = nbp + nbs
    c_all = jnp.concatenate([c_prompt, c_sample], axis=0)
    c_pad = jnp.pad(c_all, ((0, (-nc_rows) % 16), (0, 0)))
    tn_ada = _tile(6 * d, 512, 128)
    mod = _mm(c_pad, ada_w.reshape(d, 6 * d), out_dtype=F32, tm=c_pad.shape[0], tn=tn_ada, tk=d, a_fn=_silu,
              extras=[(ada_b, (1, tn_ada), lambda i, j: (0, j))], epilogue=lambda acc, b: acc + b,
              name="adaln")
    modg = jnp.concatenate([jnp.repeat(mod[:nbp], ncp, axis=0), mod[nbp:nc_rows]], axis=0)
    modg = modg.reshape(t // CHUNK, 1, 6 * d)

    h1 = _norm_mod(x, norm1_g, modg, 0, 1, tr=tok(256))
    wi = w_in.reshape(d, w_in.shape[2])
    o_mi = 2 * mh * dk + 2 * mh * dv
    o_cq = o_mi + 2 * mh
    o_za = o_cq + qr + cr + A_ROPE
    w_big = jnp.concatenate([wi[:, :o_mi], wi[:, o_za:]], axis=1).astype(BF16)
    ws_cols = qr + cr + A_ROPE + 2 * mh
    ws_pad = (-ws_cols) % 128
    w_small = jnp.concatenate([wi[:, o_cq:o_za], wi[:, o_mi:o_cq], jnp.zeros((d, ws_pad), F32)], axis=1).astype(BF16)
    zb = _mm(h1, w_big, out_dtype=BF16, tm=tok(1024), name="w_in_big")
    zs = _mm(h1, w_small, out_dtype=F32, tm=tok(512), tn=w_small.shape[1], name="w_in_small")
    wq_all, wv_all = mh * dk, mh * dv
    col_za = 2 * wq_all + 2 * wv_all
    col_zb = col_za + d

    gate_bias = jnp.concatenate([b_igate, b_fgate], axis=1)
    cqn, ckvn, krope, gates = _small_post(zs, q_norm_g, kv_norm_g, cos_t, sin_t, gate_bias,
                                          qr=qr, cr=cr, mh=mh, tr=tok(256))

    mkw = dict(mh=mh, dk=dk, dv=dv, blk_q=0, blk_k=1, blk_v=2 * wq_all // wv_all, blk_o=2 * wq_all // wv_all + 1)
    hm_p, c_p, n_p, m_p = _mlstm(zb, gates, mlstm_norm_g, None, row_blk0=0, nb=nbp, nc=ncp, **mkw)
    st = (state_C[0], state_n[0].reshape(nbs, mh, 1, dk), state_m[0].reshape(nbs, mh, 1, 1))
    hm_s, c_s, n_s, m_s = _mlstm(zb, gates, mlstm_norm_g, st, row_blk0=gp, nb=nbs, nc=1, **mkw)
    hm = jnp.concatenate([hm_p, hm_s], axis=0)
    ya = _mm(hm, w_branch_m.reshape(d, d), out_dtype=F32, tm=tok(1024), name="branch_m")

    wuq_h = jnp.transpose(w_uq[0], (1, 0, 2)).astype(BF16)
    wuk_h = jnp.transpose(w_uk[0], (1, 0, 2)).astype(BF16)
    wuv_h = jnp.transpose(w_uv[0], (1, 0, 2)).astype(BF16)
    tmh = tok(1024)
    q_hm = _mm_heads(cqn, wuq_h, n_out=dqk, out_dtype=BF16, tm=tmh, epilogue=_rope_q_epilogue,
                     extras=[(cos_t, (tmh, A_ROPE), lambda i, h: (i, 0)), (sin_t, (tmh, A_ROPE), lambda i, h: (i, 0))],
                     name="q_up")
    attn_p = _attn_prompt(q_hm, ckvn, krope, wuk_h, wuv_h, nb=nbp, seq=seq)
    attn_s = _attn_sample(q_hm, cache_ckv[0], cache_krope[0], ckvn, krope, wuk_h, wuv_h, row_blk0=gp, nb=nbs)
    attn = jnp.concatenate([attn_p, attn_s], axis=0)

    tm, tn = tok(512), _tile(d, 1024, 128)

    def merge(acc, ya_t, za_t, zb_t):
        return jax.nn.sigmoid(za_t.astype(F32)) * ya_t + jax.nn.sigmoid(zb_t.astype(F32)) * acc

    ym = _mm(attn, w_branch_a.reshape(d, d), out_dtype=BF16, tm=tm, tn=tn, epilogue=merge,
             extras=[(ya, (tm, tn), lambda i, j: (i, j)),
                     (zb, (tm, tn), lambda i, j: (i, col_za // tn + j)),
                     (zb, (tm, tn), lambda i, j: (i, col_zb // tn + j))], name="branch_a_merge")

    def residual(acc, x_t, g_t):
        r, cdim = acc.shape
        y = x_t.reshape(r // CHUNK, CHUNK, cdim) + g_t * acc.reshape(r // CHUNK, CHUNK, cdim)
        return y.reshape(r, cdim)

    x1 = _mm(ym, w_out.reshape(d, d), out_dtype=F32, tm=tm, tn=tn, epilogue=residual,
             extras=[(x, (tm, tn), lambda i, j: (i, j)),
                     (modg, (tm // CHUNK, 1, tn), lambda i, j: (i, 0, 2 * (d // tn) + j))], name="w_out")

    h2, h2_t = _norm_mod(x1, norm2_g, modg, 3, 4, tr=tok(256, 128), with_transposed=True)
    qp = _mm(h2, peer_wq.reshape(d, -1), out_dtype=BF16, tm=tok(1024), name="peer_q")
    s1, s2, e1, e2, tau = _peer_topk(qp, peer_subkeys[0].astype(BF16), tt=tok(256, 128))
    ne = peer_u.shape[1]
    pe_t = _peer_dense(h2_t, peer_u.reshape(ne, d).astype(BF16), peer_v.reshape(ne, d).T.astype(BF16),
                       s1, s2, e1, e2, tau, tt=tok(512, 128))

    gf = final_norm_g.reshape(1, d)
    y_p = _final(x1, pe_t, modg, 5, gf, row_blk0=0, rows=tp, tr=tok(256, 128)).reshape(nbp, seq, d)
    y_s = _final(x1, pe_t, modg, 5, gf, row_blk0=gp, rows=ts, tr=tok(256, 128)).reshape(nbs, dseq, d)

    return (y_p, y_s,
            ckvn[:tp].reshape(1, nbp, seq, cr), krope[:tp].reshape(1, nbp, seq, A_ROPE),
            c_p[None], n_p.reshape(1, nbp, mh, dk), m_p.reshape(1, nbp, mh),
            ckvn[tp:].reshape(1, nbs, dseq, cr), krope[tp:].reshape(1, nbs, dseq, A_ROPE),
            c_s[None], n_s.reshape(1, nbs, mh, dk), m_s.reshape(1, nbs, mh))
```

```python
import functools
import math

import jax
import jax.numpy as jnp
from jax import lax
from jax.experimental import pallas as pl
from jax.experimental.pallas import tpu as pltpu

CHUNK = 64
A_NOPE = 128
A_ROPE = 64
A_DV = 128
ROPE_THETA = 10000.0
PEER_NKEYS = 128
PEER_TOPK = 16
NORM_EPS = 1e-6

F32 = jnp.float32
BF16 = jnp.bfloat16
MIB = 1024 * 1024
NEG_INF = float("-inf")
SUBLANES = 8


def _cparams(sem, vmem_mib=48, flags=None):
    return pltpu.CompilerParams(dimension_semantics=sem, vmem_limit_bytes=vmem_mib * MIB, flags=flags)


def _tile(n, want, mult=1):
    if n <= want:
        return n
    t = (want // mult) * mult
    while t >= mult:
        if n % t == 0:
            return t
        t -= mult
    return n


def _dot(a, b):
    return jnp.dot(a, b, preferred_element_type=F32)


def _dot_nt(a, b):
    return lax.dot_general(a, b, (((1,), (1,)), ((), ())), preferred_element_type=F32)


def _silu(x):
    return x * jax.nn.sigmoid(x)


def _mm_body(*refs, nk, n_extra, a_fn, epilogue):
    a_ref, w_ref = refs[0], refs[1]
    extras = refs[2:2 + n_extra]
    o_ref = refs[2 + n_extra]
    a = a_ref[...]
    if a_fn is not None:
        a = a_fn(a)
    part = _dot(a.astype(BF16), w_ref[...].astype(BF16))

    def finish(acc):
        res = acc if epilogue is None else epilogue(acc, *[e[...] for e in extras])
        o_ref[...] = res.astype(o_ref.dtype)

    if nk == 1:
        finish(part)
        return
    acc_ref = refs[3 + n_extra]
    k = pl.program_id(2)

    @pl.when(k == 0)
    def _():
        acc_ref[...] = part

    @pl.when(k > 0)
    def _():
        acc_ref[...] += part

    @pl.when(k == nk - 1)
    def _():
        finish(acc_ref[...])


def _mm(a, w, *, out_dtype, tm=1024, tn=1024, tk=4096, extras=(), epilogue=None, a_fn=None,
        vmem_mib=48, name="mm"):
    m, kdim = a.shape
    n_out = w.shape[1]
    tm = _tile(m, tm, 8)
    tn = _tile(n_out, tn, 128)
    tk = _tile(kdim, tk, 128)
    nk = kdim // tk
    grid = (m // tm, n_out // tn, nk)
    in_specs = [pl.BlockSpec((tm, tk), lambda i, j, k: (i, k)),
                pl.BlockSpec((tk, tn), lambda i, j, k: (k, j))]
    args = [a, w]
    for arr, bshape, imap in extras:
        in_specs.append(pl.BlockSpec(bshape, functools.partial(lambda i, j, k, f: f(i, j), f=imap)))
        args.append(arr)
    scratch = [pltpu.VMEM((tm, tn), F32)] if nk > 1 else []
    return pl.pallas_call(
        functools.partial(_mm_body, nk=nk, n_extra=len(extras), a_fn=a_fn, epilogue=epilogue),
        grid=grid,
        in_specs=in_specs,
        out_specs=pl.BlockSpec((tm, tn), lambda i, j, k: (i, j)),
        out_shape=jax.ShapeDtypeStruct((m, n_out), out_dtype),
        scratch_shapes=scratch,
        compiler_params=_cparams(("parallel", "parallel", "arbitrary"), vmem_mib),
        name=name,
    )(*args)


def _mmh_body(*refs, n_extra, epilogue, hp):
    a_ref, w_ref = refs[0], refs[1]
    extras = [e[...] for e in refs[2:2 + n_extra]]
    o_ref = refs[2 + n_extra]
    a = a_ref[...]
    for hh in range(hp):
        acc = _dot(a, w_ref[hh])
        res = acc if epilogue is None else epilogue(acc, *extras)
        o_ref[hh] = res.astype(o_ref.dtype)


def _mm_heads(a, w, *, n_out, out_dtype, tm=1024, hp=4, extras=(), epilogue=None, name="mm_heads"):
    m, kdim = a.shape
    nh, _, n = w.shape
    tm = _tile(m, tm, 8)
    hp = _tile(nh, hp)
    in_specs = [pl.BlockSpec((tm, kdim), lambda i, h: (i, 0)),
                pl.BlockSpec((hp, kdim, n), lambda i, h: (h, 0, 0))]
    args = [a, w]
    for arr, bshape, imap in extras:
        in_specs.append(pl.BlockSpec(bshape, imap))
        args.append(arr)
    return pl.pallas_call(
        functools.partial(_mmh_body, n_extra=len(extras), epilogue=epilogue, hp=hp),
        grid=(m // tm, nh // hp),
        in_specs=in_specs,
        out_specs=pl.BlockSpec((hp, tm, n_out), lambda i, h: (h, i, 0)),
        out_shape=jax.ShapeDtypeStruct((nh, m, n_out), out_dtype),
        compiler_params=_cparams(("parallel", "arbitrary")),
        name=name,
    )(*args)


def _rms(x):
    return x * lax.rsqrt(jnp.mean(x * x, axis=-1, keepdims=True) + NORM_EPS)


def _norm_mod_body(*refs, n_prompt_tiles, with_residual):
    xp_ref, xs_ref, g_ref, sh_ref, sc_ref = refs[:5]
    refs = refs[5:]
    r_ref = None
    if with_residual:
        r_ref, refs = refs[0], refs[1:]
    o_ref = refs[0]
    tr, d = o_ref.shape

    def run(x_ref):
        x = x_ref[...]
        if with_residual:
            x = x + r_ref[...]
            refs[2][...] = x
        y = _rms(x) * g_ref[...]
        y = (y.reshape(tr // CHUNK, CHUNK, d) * (1.0 + sc_ref[...]) + sh_ref[...]).reshape(tr, d)
        o_ref[...] = y.astype(o_ref.dtype)
        if with_residual:
            refs[1][...] = y.T.astype(o_ref.dtype)

    i = pl.program_id(0)

    @pl.when(i < n_prompt_tiles)
    def _():
        run(xp_ref)

    @pl.when(i >= n_prompt_tiles)
    def _():
        run(xs_ref)


def _norm_mod(xp, xs, g, modg, shift_idx, scale_idx, *, tr, residual=None):
    tp, d = xp.shape
    ts = xs.shape[0]
    t = tp + ts
    npt = tp // tr
    gpt = tr // CHUNK
    with_residual = residual is not None
    row = lambda i: (i, 0)
    in_specs = [pl.BlockSpec((tr, d), lambda i: (jnp.minimum(i, npt - 1), 0)),
                pl.BlockSpec((tr, d), lambda i: (jnp.maximum(i - npt, 0), 0)),
                pl.BlockSpec((1, d), lambda i: (0, 0)),
                pl.BlockSpec((gpt, 1, d), lambda i: (i, 0, shift_idx)),
                pl.BlockSpec((gpt, 1, d), lambda i: (i, 0, scale_idx))]
    args = [xp, xs, g, modg, modg]
    out_specs = [pl.BlockSpec((tr, d), row)]
    out_shape = [jax.ShapeDtypeStruct((t, d), BF16)]
    if with_residual:
        in_specs.append(pl.BlockSpec((tr, d), row))
        args.append(residual)
        out_specs += [pl.BlockSpec((d, tr), lambda i: (0, i)), pl.BlockSpec((tr, d), row)]
        out_shape += [jax.ShapeDtypeStruct((d, t), BF16), jax.ShapeDtypeStruct((t, d), F32)]
    res = pl.pallas_call(
        functools.partial(_norm_mod_body, n_prompt_tiles=npt, with_residual=with_residual),
        grid=(t // tr,),
        in_specs=in_specs,
        out_specs=out_specs,
        out_shape=out_shape,
        compiler_params=_cparams(("arbitrary",), 56),
        name="norm_mod",
    )(*args)
    return res if with_residual else res[0]


def _rope64(x, cos, sin_signed):
    half = A_ROPE // 2
    rolled = jnp.concatenate([x[:, half:], x[:, :half]], axis=-1)
    return x * cos + rolled * sin_signed


def _log_sigmoid(x):
    return jnp.minimum(x, 0.0) - jnp.log1p(jnp.exp(-jnp.abs(x)))


def _small_post_body(z_ref, qg_ref, kvg_ref, cos_ref, sin_ref, bias_ref,
                     cq_ref, ckv_ref, kr_ref, gate_ref, *, qr, cr, mh):
    z = z_ref[...]
    cq_ref[...] = (_rms(z[:, :qr]) * qg_ref[...]).astype(cq_ref.dtype)
    ckv_ref[...] = _rms(z[:, qr:qr + cr]) * kvg_ref[...]
    o = qr + cr
    kr_ref[...] = _rope64(z[:, o:o + A_ROPE], cos_ref[...], sin_ref[...])
    o += A_ROPE
    pre = z[:, o:o + 2 * mh] + bias_ref[...]
    lane = lax.broadcasted_iota(jnp.int32, pre.shape, 1)
    gate_ref[...] = jnp.where(lane < mh, pre, _log_sigmoid(pre))


def _small_post(zs, qg, kvg, cos, sin_s, gate_bias, *, qr, cr, mh, tr):
    t, w = zs.shape
    row = lambda i: (i, 0)
    fix = lambda i: (0, 0)
    return pl.pallas_call(
        functools.partial(_small_post_body, qr=qr, cr=cr, mh=mh),
        grid=(t // tr,),
        in_specs=[pl.BlockSpec((tr, w), row), pl.BlockSpec((1, qr), fix), pl.BlockSpec((1, cr), fix),
                  pl.BlockSpec((tr, A_ROPE), row), pl.BlockSpec((tr, A_ROPE), row),
                  pl.BlockSpec((1, 2 * mh), fix)],
        out_specs=[pl.BlockSpec((tr, qr), row), pl.BlockSpec((tr, cr), row),
                   pl.BlockSpec((tr, A_ROPE), row), pl.BlockSpec((tr, 2 * mh), row)],
        out_shape=[jax.ShapeDtypeStruct((t, qr), BF16), jax.ShapeDtypeStruct((t, cr), F32),
                   jax.ShapeDtypeStruct((t, A_ROPE), F32), jax.ShapeDtypeStruct((t, 2 * mh), F32)],
        compiler_params=_cparams(("parallel",)),
        name="small_post",
    )(zs, qg, kvg, cos, sin_s, gate_bias)


def _mlstm_head(q, k, v, mo, ig_col, lf_col, g, c_prev, n_prev, m_prev, scale):
    L = q.shape[0]
    r = lax.broadcasted_iota(jnp.int32, (L, L), 0)
    c = lax.broadcasted_iota(jnp.int32, (L, L), 1)
    eye = r == c
    ig_row = jnp.sum(jnp.where(eye, ig_col, 0.0), axis=0, keepdims=True)
    lf_row = jnp.sum(jnp.where(eye, lf_col, 0.0), axis=0, keepdims=True)
    causal = c <= r
    b_col = jnp.sum(jnp.where(causal, lf_row, 0.0), axis=1, keepdims=True)
    b_row = jnp.sum(jnp.where(r <= c, lf_col, 0.0), axis=0, keepdims=True)
    b_last = jnp.sum(lf_col, axis=0, keepdims=True)

    a_col = b_col + m_prev
    dmat = jnp.where(causal, b_col - b_row + ig_row, NEG_INF)
    m_t = jnp.maximum(a_col, jnp.max(dmat, axis=1, keepdims=True))
    w_inter = jnp.exp(a_col - m_t)
    w_intra = jnp.exp(dmat - m_t)

    qs = q.astype(F32) * scale
    qb = qs.astype(BF16)
    s = _dot_nt(qb, k) * w_intra
    num = _dot(s.astype(BF16), v) + w_inter * _dot(qb, c_prev.astype(BF16))
    den = jnp.sum(s, axis=1, keepdims=True) + w_inter * jnp.sum(qs * n_prev, axis=1, keepdims=True)
    h = num / jnp.maximum(jnp.abs(den), jnp.exp(-m_t))
    h_out = _rms(h) * g * jax.nn.sigmoid(mo.astype(F32))

    m_new = jnp.sum(jnp.where(lax.broadcasted_iota(jnp.int32, (L, 1), 0) == L - 1, m_t, 0.0),
                    axis=0, keepdims=True)
    w_c = jnp.exp(b_last + m_prev - m_new)
    w_s = jnp.exp(b_last - b_col + ig_col - m_new)
    kw = k.astype(F32) * w_s
    c_new = w_c * c_prev + _dot(kw.T.astype(BF16), v)
    n_new = w_c * n_prev + jnp.sum(kw, axis=0, keepdims=True)
    return h_out, c_new, n_new, m_new


def _mlstm_body(*refs, has_state, mh, dk, dv, scale):
    if has_state:
        (q_ref, k_ref, v_ref, mo_ref, gate_ref, g_ref, c0_ref, n0_ref, m0_ref,
         h_ref, c_ref, n_ref, m_ref) = refs
    else:
        (q_ref, k_ref, v_ref, mo_ref, gate_ref, g_ref, h_ref, c_ref, n_ref, m_ref) = refs
    ci = pl.program_id(1)

    @pl.when(ci == 0)
    def _():
        if has_state:
            c_ref[...] = c0_ref[...]
            n_ref[...] = n0_ref[...]
            m_ref[...] = m0_ref[...]
        else:
            c_ref[...] = jnp.zeros_like(c_ref)
            n_ref[...] = jnp.zeros_like(n_ref)
            m_ref[...] = jnp.zeros_like(m_ref)

    gates = gate_ref[...]
    for hd in range(mh):
        h_out, c_new, n_new, m_new = _mlstm_head(
            q_ref[:, hd * dk:(hd + 1) * dk], k_ref[:, hd * dk:(hd + 1) * dk],
            v_ref[:, hd * dv:(hd + 1) * dv], mo_ref[:, hd * dv:(hd + 1) * dv],
            gates[:, hd:hd + 1], gates[:, mh + hd:mh + hd + 1], g_ref[:, hd * dv:(hd + 1) * dv],
            c_ref[0, hd], n_ref[0, hd], m_ref[0, hd], scale)
        h_ref[:, hd * dv:(hd + 1) * dv] = h_out.astype(h_ref.dtype)
        c_ref[0, hd] = c_new
        n_ref[0, hd] = n_new
        m_ref[0, hd] = m_new


def _mlstm(zb, gates, norm_g, state, *, row_blk0, nb, nc, mh, dk, dv, blk_q, blk_k, blk_v, blk_o):
    has_state = state is not None
    wq, wv = mh * dk, mh * dv
    rowmap = lambda blk: (lambda b, c: (row_blk0 + b * nc + c, blk))
    in_specs = [pl.BlockSpec((CHUNK, wq), rowmap(blk_q)),
                pl.BlockSpec((CHUNK, wq), rowmap(blk_k)),
                pl.BlockSpec((CHUNK, wv), rowmap(blk_v)),
                pl.BlockSpec((CHUNK, wv), rowmap(blk_o)),
                pl.BlockSpec((CHUNK, 2 * mh), rowmap(0)),
                pl.BlockSpec((1, wv), lambda b, c: (0, 0))]
    args = [zb, zb, zb, zb, gates, norm_g]
    st_specs = [pl.BlockSpec((1, mh, dk, dv), lambda b, c: (b, 0, 0, 0)),
                pl.BlockSpec((1, mh, 1, dk), lambda b, c: (b, 0, 0, 0)),
                pl.BlockSpec((1, mh, 1, 1), lambda b, c: (b, 0, 0, 0))]
    if has_state:
        in_specs += st_specs
        args += list(state)
    return pl.pallas_call(
        functools.partial(_mlstm_body, has_state=has_state, mh=mh, dk=dk, dv=dv, scale=dk ** -0.5),
        grid=(nb, nc),
        in_specs=in_specs,
        out_specs=[pl.BlockSpec((CHUNK, wv), lambda b, c: (b * nc + c, 0))] + st_specs,
        out_shape=[jax.ShapeDtypeStruct((nb * nc * CHUNK, wv), BF16),
                   jax.ShapeDtypeStruct((nb, mh, dk, dv), F32),
                   jax.ShapeDtypeStruct((nb, mh, 1, dk), F32),
                   jax.ShapeDtypeStruct((nb, mh, 1, 1), F32)],
        compiler_params=_cparams(("parallel", "arbitrary")),
        name="mlstm",
    )(*args)


def _attn_prompt_body(q_ref, ckv_ref, kr_ref, wuk_ref, wuv_ref, o_ref, k_scr, v_scr, *, tq, scale, hp):
    qi = pl.program_id(2)

    @pl.when(qi == 0)
    def _():
        ckv = ckv_ref[...].astype(BF16)
        kr = kr_ref[...].astype(BF16)
        for hh in range(hp):
            k_scr[hh, :, :A_NOPE] = _dot(ckv, wuk_ref[hh]).astype(BF16)
            k_scr[hh, :, A_NOPE:] = kr
            v_scr[hh] = _dot(ckv, wuv_ref[hh]).astype(BF16)

    def step(kb, carry, diagonal):
        rows = pl.ds(pl.multiple_of(kb * tq, tq), tq)
        out = []
        for hh in range(hp):
            m, l, acc = carry[hh]
            s = _dot_nt(q_ref[hh], k_scr[hh, rows, :]) * scale
            if diagonal:
                qc = lax.broadcasted_iota(jnp.int32, (tq, tq), 0) // CHUNK
                kc = lax.broadcasted_iota(jnp.int32, (tq, tq), 1) // CHUNK
                s = jnp.where(kc <= qc, s, NEG_INF)
            m_new = jnp.maximum(m, jnp.max(s, axis=1, keepdims=True))
            alpha = jnp.exp(m - m_new)
            p = jnp.exp(s - m_new)
            l = alpha * l + jnp.sum(p, axis=1, keepdims=True)
            acc = alpha * acc + _dot(p.astype(BF16), v_scr[hh, rows, :])
            out.append((m_new, l, acc))
        return tuple(out)

    init = tuple((jnp.full((tq, 1), NEG_INF, F32), jnp.zeros((tq, 1), F32), jnp.zeros((tq, A_DV), F32))
                 for _ in range(hp))
    carry = lax.fori_loop(0, qi, lambda kb, c: step(kb, c, False), init)
    final = step(qi, carry, True)
    for hh in range(hp):
        _, l, acc = final[hh]
        o_ref[:, hh * A_DV:(hh + 1) * A_DV] = (acc / l).astype(o_ref.dtype)


def _attn_prompt(q_hm, ckvn, krope, wuk_h, wuv_h, *, nb, seq, tq=512, hp=2):
    nh, _, dqk = q_hm.shape
    cr = ckvn.shape[1]
    tq = _tile(seq, tq, CHUNK)
    nq = seq // tq
    hp = _tile(nh, hp)
    return pl.pallas_call(
        functools.partial(_attn_prompt_body, tq=tq, scale=dqk ** -0.5, hp=hp),
        grid=(nb, nh // hp, nq),
        in_specs=[pl.BlockSpec((hp, tq, dqk), lambda b, h, i: (h, b * nq + i, 0)),
                  pl.BlockSpec((seq, cr), lambda b, h, i: (b, 0)),
                  pl.BlockSpec((seq, A_ROPE), lambda b, h, i: (b, 0)),
                  pl.BlockSpec((hp, cr, A_NOPE), lambda b, h, i: (h, 0, 0)),
                  pl.BlockSpec((hp, cr, A_DV), lambda b, h, i: (h, 0, 0))],
        out_specs=pl.BlockSpec((tq, hp * A_DV), lambda b, h, i: (b * nq + i, h)),
        out_shape=jax.ShapeDtypeStruct((nb * seq, nh * A_DV), BF16),
        scratch_shapes=[pltpu.VMEM((hp, seq, dqk), BF16), pltpu.VMEM((hp, seq, A_DV), BF16)],
        compiler_params=_cparams(("parallel", "arbitrary", "arbitrary")),
        name="attn_prompt",
    )(q_hm, ckvn, krope, wuk_h, wuv_h)


def _attn_sample_body(q_ref, cckv_ref, ckr_ref, nckv_ref, nkr_ref, wuk_ref, wuv_ref, o_ref, qlat_ref,
                      *, hg, past, scale, all_allowed):
    L = q_ref.shape[1]
    for hh in range(hg):
        qlat_ref[hh * L:(hh + 1) * L, :] = _dot_nt(q_ref[hh, :, :A_NOPE], wuk_ref[hh]).astype(BF16)
    qlat = qlat_ref[...]
    qr = q_ref[:, :, A_NOPE:].reshape(hg * L, A_ROPE)
    ckv_p = cckv_ref[0].astype(BF16)
    ckv_n = nckv_ref[...].astype(BF16)
    s_p = (_dot_nt(qlat, ckv_p) + _dot_nt(qr, ckr_ref[0].astype(BF16))) * scale
    s_n = (_dot_nt(qlat, ckv_n) + _dot_nt(qr, nkr_ref[...].astype(BF16))) * scale
    if not all_allowed:
        def masked(s, k0):
            qpos = past + lax.broadcasted_iota(jnp.int32, s.shape, 0) % L
            kpos = k0 + lax.broadcasted_iota(jnp.int32, s.shape, 1)
            return jnp.where((kpos // CHUNK) <= (qpos // CHUNK), s, NEG_INF)

        s_p = masked(s_p, 0)
        s_n = masked(s_n, past)
    m = jnp.maximum(jnp.max(s_p, axis=1, keepdims=True), jnp.max(s_n, axis=1, keepdims=True))
    p_p = jnp.exp(s_p - m)
    p_n = jnp.exp(s_n - m)
    l = jnp.sum(p_p, axis=1, keepdims=True) + jnp.sum(p_n, axis=1, keepdims=True)
    o = (_dot(p_p.astype(BF16), ckv_p) + _dot(p_n.astype(BF16), ckv_n)) / l
    ob = o.astype(BF16)
    for hh in range(hg):
        o_ref[:, hh * A_DV:(hh + 1) * A_DV] = _dot(ob[hh * L:(hh + 1) * L, :], wuv_ref[hh]).astype(o_ref.dtype)


def _attn_sample(q_hm, cache_ckv, cache_kr, ckvn, krope, wuk_h, wuv_h, *, row_blk0, nb, hg=8):
    nh = q_hm.shape[0]
    _, past, cr = cache_ckv.shape
    hg = _tile(nh, hg)
    dqk = A_NOPE + A_ROPE
    all_allowed = (past + CHUNK - 1) // CHUNK <= past // CHUNK
    return pl.pallas_call(
        functools.partial(_attn_sample_body, hg=hg, past=past, scale=dqk ** -0.5, all_allowed=all_allowed),
        grid=(nb, nh // hg),
        in_specs=[pl.BlockSpec((hg, CHUNK, dqk), lambda b, j: (j, row_blk0 + b, 0)),
                  pl.BlockSpec((1, past, cr), lambda b, j: (b, 0, 0)),
                  pl.BlockSpec((1, past, A_ROPE), lambda b, j: (b, 0, 0)),
                  pl.BlockSpec((CHUNK, cr), lambda b, j: (row_blk0 + b, 0)),
                  pl.BlockSpec((CHUNK, A_ROPE), lambda b, j: (row_blk0 + b, 0)),
                  pl.BlockSpec((hg, cr, A_NOPE), lambda b, j: (j, 0, 0)),
                  pl.BlockSpec((hg, cr, A_DV), lambda b, j: (j, 0, 0))],
        out_specs=pl.BlockSpec((CHUNK, hg * A_DV), lambda b, j: (b, j)),
        out_shape=jax.ShapeDtypeStruct((nb * CHUNK, nh * A_DV), BF16),
        scratch_shapes=[pltpu.VMEM((hg * CHUNK, cr), BF16)],
        compiler_params=_cparams(("parallel", "arbitrary")),
        name="attn_sample",
    )(q_hm, cache_ckv, cache_kr, ckvn, krope, wuk_h, wuv_h)


def _extract_topk(x, vals_ref, base):
    n = x.shape[0]
    row = lax.broadcasted_iota(jnp.int32, x.shape, 0)

    def body(it, x):
        m = jnp.max(x, axis=0, keepdims=True)
        vals_ref[pl.ds(base + it, 1), :] = m
        first = jnp.min(jnp.where(x == m, row, n), axis=0, keepdims=True)
        return jnp.where(row == first, NEG_INF, x)

    lax.fori_loop(0, PEER_TOPK, body, x)


def _cand_layout():
    kk = PEER_TOPK
    groups, a = [], 0
    while kk // (a + 1) > 1:
        nv = kk // (a + 1)
        groups.append((a, nv, -(-nv // SUBLANES) * SUBLANES))
        a += 1
    return groups, a


def _peer_topk_body(qp_ref, sk_ref, s1_ref, s2_ref, e1_ref, e2_ref, tau_ref, vals_ref, cand_ref, *, nh):
    kk = PEER_TOPK
    nkeys = PEER_NKEYS
    groups, tail = _cand_layout()
    for h in range(nh):
        s1 = _dot_nt(sk_ref[h, 0], qp_ref[:, (2 * h) * nkeys:(2 * h + 1) * nkeys])
        s2 = _dot_nt(sk_ref[h, 1], qp_ref[:, (2 * h + 1) * nkeys:(2 * h + 2) * nkeys])
        s1_ref[h] = s1
        s2_ref[h] = s2
        _extract_topk(s1, vals_ref, 0)
        _extract_topk(s2, vals_ref, kk)
        r0 = 0
        for a, nv, nr in groups:
            blk = vals_ref[a:a + 1, :] + vals_ref[kk:kk + nr, :]
            if nv < nr:
                blk = jnp.where(lax.broadcasted_iota(jnp.int32, blk.shape, 0) < nv, blk, NEG_INF)
            cand_ref[r0:r0 + nr, :] = blk
            r0 += nr
        cand_ref[r0:r0 + kk - tail, :] = vals_ref[tail:kk, :] + vals_ref[kk:kk + 1, :]
        max1 = vals_ref[0:1, :]
        max2 = vals_ref[kk:kk + 1, :]
        _extract_topk(cand_ref[...], vals_ref, 2 * kk)
        best = vals_ref[2 * kk:3 * kk, :]
        zsum = jnp.sum(jnp.exp(best - best[0:1, :]), axis=0, keepdims=True)
        tau_ref[h:h + 1, :] = best[kk - 1:kk, :]
        e1_ref[h] = jnp.exp(s1 - max1)
        e2_ref[h] = jnp.exp(s2 - max2) / zsum


def _peer_topk(qp, subkeys, *, tt):
    t = qp.shape[0]
    nh = subkeys.shape[0]
    nkeys = PEER_NKEYS
    groups, tail = _cand_layout()
    n_cand = sum(nr for _, _, nr in groups) + PEER_TOPK - tail
    stat = jax.ShapeDtypeStruct((nh, nkeys, t), F32)
    stat_spec = pl.BlockSpec((nh, nkeys, tt), lambda i: (0, 0, i))
    return pl.pallas_call(
        functools.partial(_peer_topk_body, nh=nh),
        grid=(t // tt,),
        in_specs=[pl.BlockSpec((tt, 2 * nh * nkeys), lambda i: (i, 0)),
                  pl.BlockSpec(subkeys.shape, lambda i: (0, 0, 0, 0))],
        out_specs=[stat_spec, stat_spec, stat_spec, stat_spec, pl.BlockSpec((nh, tt), lambda i: (0, i))],
        out_shape=[stat, stat, stat, stat, jax.ShapeDtypeStruct((nh, t), F32)],
        scratch_shapes=[pltpu.VMEM((4 * PEER_TOPK, tt), F32), pltpu.VMEM((n_cand, tt), F32)],
        compiler_params=_cparams(("parallel",)),
        name="peer_topk",
    )(qp, subkeys)


def _gelu(x):
    return 0.5 * x * (1.0 + lax.erf(x * (1.0 / math.sqrt(2.0))))


def _peer_dense_body(ht_ref, u_ref, vt_ref, s1_ref, s2_ref, e1_ref, e2_ref, tau_ref, o_ref,
                     act_a, act_b, coef_a, coef_b, *, nh, ni, n_et, dchunk):
    e = pl.program_id(1)
    nkeys = PEER_NKEYS
    d, tt = o_ref.shape
    lanes = 128

    @pl.when(e == 0)
    def _():
        o_ref[...] = jnp.zeros_like(o_ref)
        act_b[...] = jnp.zeros_like(act_b)
        coef_b[...] = jnp.zeros_like(coef_b)

    tile_c = jnp.clip(e - 1, 0, n_et - 1)

    def step(act_w, act_r, coef_w, coef_r):
        def activations():
            act_w[...] = _dot(u_ref[...], ht_ref[...])

        def coefficients(ii, c0):
            i = tile_c * ni + ii
            rows, cols = slice(ii * nkeys, (ii + 1) * nkeys), slice(c0, c0 + lanes)
            gate = None
            for h in range(nh):
                s1row = s1_ref[h, pl.ds(i, 1), :][:, cols]
                e1row = e1_ref[h, pl.ds(i, 1), :][:, cols]
                hit = (s1row + s2_ref[h, :, cols]) >= tau_ref[h:h + 1, cols]
                term = jnp.where(hit, e1row * e2_ref[h, :, cols], 0.0)
                gate = term if gate is None else gate + term
            coef_w[rows, cols] = (gate * _gelu(act_r[rows, cols])).astype(coef_w.dtype)

        def drain(r0):
            o_ref[r0:r0 + dchunk, :] += _dot(vt_ref[0, r0:r0 + dchunk, :], coef_r[...])

        mxu_items = [functools.partial(drain, r0) for r0 in range(0, d, dchunk)]
        mxu_items.insert(len(mxu_items) // 2, activations)
        vpu_items = [functools.partial(coefficients, ii, c0) for ii in range(ni) for c0 in range(0, tt, lanes)]
        n_m, n_v = len(mxu_items), len(vpu_items)
        done_v = 0
        for k, item in enumerate(mxu_items):
            item()
            upto = (k + 1) * n_v // n_m
            for v in vpu_items[done_v:upto]:
                v()
            done_v = upto

    @pl.when(e % 2 == 0)
    def _():
        step(act_a, act_b, coef_a, coef_b)

    @pl.when(e % 2 == 1)
    def _():
        step(act_b, act_a, coef_b, coef_a)


def _peer_dense(h2_t, u_tab, vt_tiles, s1, s2, e1, e2, tau, *, tt, dchunk=512):
    d, t = h2_t.shape
    ne = u_tab.shape[0]
    nh = s1.shape[0]
    nkeys = PEER_NKEYS
    n_et, _, te = vt_tiles.shape
    ni = te // nkeys
    dchunk = _tile(d, dchunk, 8)
    once = pl.Buffered(1)
    stat_spec = pl.BlockSpec((nh, nkeys, tt), lambda i, e: (0, 0, i), pipeline_mode=once)
    return pl.pallas_call(
        functools.partial(_peer_dense_body, nh=nh, ni=ni, n_et=n_et, dchunk=dchunk),
        grid=(t // tt, n_et + 2),
        in_specs=[pl.BlockSpec((d, tt), lambda i, e: (0, i), pipeline_mode=once),
                  pl.BlockSpec((te, d), lambda i, e: (jnp.minimum(e, n_et - 1), 0)),
                  pl.BlockSpec((1, d, te), lambda i, e: (jnp.clip(e - 2, 0, n_et - 1), 0, 0)),
                  stat_spec, stat_spec, stat_spec, stat_spec,
                  pl.BlockSpec((nh, tt), lambda i, e: (0, i), pipeline_mode=once)],
        out_specs=pl.BlockSpec((d, tt), lambda i, e: (0, i)),
        out_shape=jax.ShapeDtypeStruct((d, t), F32),
        scratch_shapes=[pltpu.VMEM((te, tt), F32), pltpu.VMEM((te, tt), F32),
                        pltpu.VMEM((te, tt), BF16), pltpu.VMEM((te, tt), BF16)],
        compiler_params=_cparams(("parallel", "arbitrary"), 56),
        name="peer_dense",
    )(h2_t, u_tab, vt_tiles, s1, s2, e1, e2, tau)


def _final_body(x_ref, pt_ref, g2_ref, gf_ref, o_ref):
    tr, d = x_ref.shape
    p = pt_ref[...].T
    x = x_ref[...].reshape(tr // CHUNK, CHUNK, d) + g2_ref[...] * p.reshape(tr // CHUNK, CHUNK, d)
    o_ref[...] = (_rms(x.reshape(tr, d)) * gf_ref[...]).astype(o_ref.dtype)


def _final(x1, pe_t, modg, gate_idx, gf, *, row_blk0, rows, tr):
    d = x1.shape[1]
    gpt = tr // CHUNK
    r0 = row_blk0 * CHUNK // tr
    return pl.pallas_call(
        _final_body,
        grid=(rows // tr,),
        in_specs=[pl.BlockSpec((tr, d), lambda i: (r0 + i, 0)),
                  pl.BlockSpec((d, tr), lambda i: (0, r0 + i)),
                  pl.BlockSpec((gpt, 1, d), lambda i: (r0 + i, 0, gate_idx)),
                  pl.BlockSpec((1, d), lambda i: (0, 0))],
        out_specs=pl.BlockSpec((tr, d), lambda i: (i, 0)),
        out_shape=jax.ShapeDtypeStruct((rows, d), F32),
        compiler_params=_cparams(("parallel",)),
        name="final_norm",
    )(x1, pe_t, modg, gf)


def _rope_tables(pos):
    half = A_ROPE // 2
    inv = ROPE_THETA ** (-jnp.arange(half, dtype=F32) / half)
    ang = pos.astype(F32)[:, None] * inv[None, :]
    cos, sin = jnp.cos(ang), jnp.sin(ang)
    return jnp.concatenate([cos, cos], axis=-1), jnp.concatenate([-sin, sin], axis=-1)


def _rope_q_epilogue(acc, cos, sin_s):
    return jnp.concatenate([acc[:, :A_NOPE], _rope64(acc[:, A_NOPE:], cos, sin_s)], axis=-1)


def kernel(x_prompt, x_sample, c_prompt, c_sample, cache_ckv, cache_krope, state_C, state_n, state_m, ada_w, ada_b, norm1_g, w_in, b_igate, b_fgate, mlstm_norm_g, q_norm_g, kv_norm_g, w_uq, w_uk, w_uv, w_branch_m, w_branch_a, w_out, norm2_g, peer_wq, peer_subkeys, peer_u, peer_v, final_norm_g):
    nbp, seq, d = x_prompt.shape
    nbs, dseq, _ = x_sample.shape
    depth = ada_w.shape[0]
    assert depth == 1 and seq % CHUNK == 0 and dseq == CHUNK
    past = cache_ckv.shape[2]
    mh, dk, dv = state_C.shape[2:]
    qr, ah, dqk = w_uq.shape[1:]
    cr = w_uk.shape[1]
    assert dqk == A_NOPE + A_ROPE and w_uv.shape[3] == A_DV and ah * A_DV == d and mh * dv == d
    assert (2 * dk) % dv == 0
    assert peer_subkeys.shape[3] == PEER_NKEYS and peer_subkeys.shape[4] == PEER_NKEYS
    tp, ts = nbp * seq, nbs * dseq
    t = tp + ts
    gp = tp // CHUNK
    ncp = seq // CHUNK

    def tok(want, mult=CHUNK):
        return _tile(math.gcd(tp, ts), want, mult)

    xp, xs = x_prompt.reshape(tp, d), x_sample.reshape(ts, d)
    pos = jnp.concatenate([jnp.tile(jnp.arange(seq), nbp), jnp.tile(past + jnp.arange(dseq), nbs)])
    cos_t, sin_t = _rope_tables(pos)

    nc_rows = nbp + nbs
    c_all = jnp.concatenate([c_prompt, c_sample], axis=0)
    c_pad = jnp.pad(c_all, ((0, (-nc_rows) % 16), (0, 0)))
    tn_ada = _tile(6 * d, 512, 128)
    mod = _mm(c_pad, ada_w.reshape(d, 6 * d), out_dtype=F32, tm=c_pad.shape[0], tn=tn_ada, tk=d, a_fn=_silu,
              extras=[(ada_b, (1, tn_ada), lambda i, j: (0, j))], epilogue=lambda acc, b: acc + b,
              name="adaln")
    modg = jnp.concatenate([jnp.repeat(mod[:nbp], ncp, axis=0), mod[nbp:nc_rows]], axis=0)
    modg = modg.reshape(t // CHUNK, 1, 6 * d)

    h1 = _norm_mod(xp, xs, norm1_g, modg, 0, 1, tr=tok(256))
    wi = w_in.reshape(d, w_in.shape[2])
    o_mi = 2 * mh * dk + 2 * mh * dv
    o_cq = o_mi + 2 * mh
    o_za = o_cq + qr + cr + A_ROPE
    ws_cols = qr + cr + A_ROPE + 2 * mh
    ws_pad = (-ws_cols) % 128
    w_small = jnp.concatenate([wi[:, o_cq:o_za], wi[:, o_mi:o_cq], jnp.zeros((d, ws_pad), F32)], axis=1).astype(BF16)
    zm = _mm(h1, wi[:, :o_mi].astype(BF16), out_dtype=BF16, tm=tok(1024), name="w_in_mlstm")
    zg = _mm(h1, wi[:, o_za:].astype(BF16), out_dtype=BF16, tm=tok(1024), name="w_in_gates")
    zs = _mm(h1, w_small, out_dtype=F32, tm=tok(512), tn=w_small.shape[1], name="w_in_small")
    wq_all, wv_all = mh * dk, mh * dv

    gate_bias = jnp.concatenate([b_igate, b_fgate], axis=1)
    cqn, ckvn, krope, gates = _small_post(zs, q_norm_g, kv_norm_g, cos_t, sin_t, gate_bias,
                                          qr=qr, cr=cr, mh=mh, tr=tok(256))

    mkw = dict(mh=mh, dk=dk, dv=dv, blk_q=0, blk_k=1, blk_v=2 * wq_all // wv_all, blk_o=2 * wq_all // wv_all + 1)
    hm_p, c_p, n_p, m_p = _mlstm(zm, gates, mlstm_norm_g, None, row_blk0=0, nb=nbp, nc=ncp, **mkw)
    st = (state_C[0], state_n[0].reshape(nbs, mh, 1, dk), state_m[0].reshape(nbs, mh, 1, 1))
    hm_s, c_s, n_s, m_s = _mlstm(zm, gates, mlstm_norm_g, st, row_blk0=gp, nb=nbs, nc=1, **mkw)
    hm = jnp.concatenate([hm_p, hm_s], axis=0)
    ya = _mm(hm, w_branch_m.reshape(d, d).astype(BF16), out_dtype=F32, tm=tok(1024), name="branch_m")

    wuq_h = jnp.transpose(w_uq[0], (1, 0, 2)).astype(BF16)
    wuk_h = jnp.transpose(w_uk[0], (1, 0, 2)).astype(BF16)
    wuv_h = jnp.transpose(w_uv[0], (1, 0, 2)).astype(BF16)
    tmh = tok(1024)
    q_hm = _mm_heads(cqn, wuq_h, n_out=dqk, out_dtype=BF16, tm=tmh, epilogue=_rope_q_epilogue,
                     extras=[(cos_t, (tmh, A_ROPE), lambda i, h: (i, 0)), (sin_t, (tmh, A_ROPE), lambda i, h: (i, 0))],
                     name="q_up")
    attn_p = _attn_prompt(q_hm, ckvn, krope, wuk_h, wuv_h, nb=nbp, seq=seq)
    attn_s = _attn_sample(q_hm, cache_ckv[0], cache_krope[0], ckvn, krope, wuk_h, wuv_h, row_blk0=gp, nb=nbs)
    attn = jnp.concatenate([attn_p, attn_s], axis=0)

    tm, tn = tok(512), _tile(d, 1024, 128)

    def merge(acc, ya_t, za_t, zb_t):
        return jax.nn.sigmoid(za_t.astype(F32)) * ya_t + jax.nn.sigmoid(zb_t.astype(F32)) * acc

    ym = _mm(attn, w_branch_a.reshape(d, d).astype(BF16), out_dtype=BF16, tm=tm, tn=tn, epilogue=merge,
             extras=[(ya, (tm, tn), lambda i, j: (i, j)),
                     (zg, (tm, tn), lambda i, j: (i, j)),
                     (zg, (tm, tn), lambda i, j: (i, d // tn + j))], name="branch_a_merge")

    def gated(acc, g_t):
        r, cdim = acc.shape
        return (g_t * acc.reshape(r // CHUNK, CHUNK, cdim)).reshape(r, cdim)

    r1 = _mm(ym, w_out.reshape(d, d).astype(BF16), out_dtype=F32, tm=tm, tn=tn, epilogue=gated,
             extras=[(modg, (tm // CHUNK, 1, tn), lambda i, j: (i, 0, 2 * (d // tn) + j))], name="w_out")

    h2, h2_t, x1 = _norm_mod(xp, xs, norm2_g, modg, 3, 4, tr=tok(256, 128), residual=r1)
    qp = _mm(h2, peer_wq.reshape(d, -1).astype(BF16), out_dtype=BF16, tm=tok(1024), name="peer_q")
    s1, s2, e1, e2, tau = _peer_topk(qp, peer_subkeys[0].astype(BF16), tt=tok(256, 128))
    ne = peer_u.shape[1]
    te = _tile(ne, 512, PEER_NKEYS)
    vt_tiles = jnp.transpose(peer_v.reshape(ne // te, te, d), (0, 2, 1)).astype(BF16)
    pe_t = _peer_dense(h2_t, peer_u.reshape(ne, d).astype(BF16), vt_tiles,
                       s1, s2, e1, e2, tau, tt=tok(512, 128))

    gf = final_norm_g.reshape(1, d)
    y_p = _final(x1, pe_t, modg, 5, gf, row_blk0=0, rows=tp, tr=tok(256, 128)).reshape(nbp, seq, d)
    y_s = _final(x1, pe_t, modg, 5, gf, row_blk0=gp, rows=ts, tr=tok(256, 128)).reshape(nbs, dseq, d)

    return (y_p, y_s,
            ckvn[:tp].reshape(1, nbp, seq, cr), krope[:tp].reshape(1, nbp, seq, A_ROPE),
            c_p[None], n_p.reshape(1, nbp, mh, dk), m_p.reshape(1, nbp, mh),
            ckvn[tp:].reshape(1, nbs, dseq, cr), krope[tp:].reshape(1, nbs, dseq, A_ROPE),
            c_s[None], n_s.reshape(1, nbs, mh, dk), m_s.reshape(1, nbs, mh))
```

```python
import functools
import math

import jax
import jax.numpy as jnp
from jax import lax
from jax.experimental import pallas as pl
from jax.experimental.pallas import tpu as pltpu

CHUNK = 64
A_NOPE = 128
A_ROPE = 64
A_DV = 128
ROPE_THETA = 10000.0
PEER_NKEYS = 128
PEER_TOPK = 16
NORM_EPS = 1e-6

F32 = jnp.float32
BF16 = jnp.bfloat16
MIB = 1024 * 1024
NEG_INF = float("-inf")
SUBLANES = 8


def _cparams(sem, vmem_mib=48):
    return pltpu.CompilerParams(dimension_semantics=sem, vmem_limit_bytes=vmem_mib * MIB)


def _tile(n, want, mult=1):
    if n <= want:
        return n
    t = (want // mult) * mult
    while t >= mult:
        if n % t == 0:
            return t
        t -= mult
    return n


def _dot(a, b):
    return jnp.dot(a, b, preferred_element_type=F32)


def _dot_nt(a, b):
    return lax.dot_general(a, b, (((1,), (1,)), ((), ())), preferred_element_type=F32)


def _silu(x):
    return x * jax.nn.sigmoid(x)


def _mm_body(*refs, nk, n_extra, a_fn, epilogue):
    a_ref, w_ref = refs[0], refs[1]
    extras = refs[2:2 + n_extra]
    o_ref = refs[2 + n_extra]
    a = a_ref[...]
    if a_fn is not None:
        a = a_fn(a)
    part = _dot(a.astype(BF16), w_ref[...].astype(BF16))

    def finish(acc):
        res = acc if epilogue is None else epilogue(acc, *[e[...] for e in extras])
        o_ref[...] = res.astype(o_ref.dtype)

    if nk == 1:
        finish(part)
        return
    acc_ref = refs[3 + n_extra]
    k = pl.program_id(2)

    @pl.when(k == 0)
    def _():
        acc_ref[...] = part

    @pl.when(k > 0)
    def _():
        acc_ref[...] += part

    @pl.when(k == nk - 1)
    def _():
        finish(acc_ref[...])


def _mm(a, w, *, out_dtype, tm=1024, tn=1024, tk=4096, extras=(), epilogue=None, a_fn=None,
        vmem_mib=48, name="mm"):
    m, kdim = a.shape
    n_out = w.shape[1]
    tm = _tile(m, tm, 8)
    tn = _tile(n_out, tn, 128)
    tk = _tile(kdim, tk, 128)
    nk = kdim // tk
    grid = (m // tm, n_out // tn, nk)
    in_specs = [pl.BlockSpec((tm, tk), lambda i, j, k: (i, k)),
                pl.BlockSpec((tk, tn), lambda i, j, k: (k, j))]
    args = [a, w]
    for arr, bshape, imap in extras:
        in_specs.append(pl.BlockSpec(bshape, functools.partial(lambda i, j, k, f: f(i, j), f=imap)))
        args.append(arr)
    scratch = [pltpu.VMEM((tm, tn), F32)] if nk > 1 else []
    return pl.pallas_call(
        functools.partial(_mm_body, nk=nk, n_extra=len(extras), a_fn=a_fn, epilogue=epilogue),
        grid=grid,
        in_specs=in_specs,
        out_specs=pl.BlockSpec((tm, tn), lambda i, j, k: (i, j)),
        out_shape=jax.ShapeDtypeStruct((m, n_out), out_dtype),
        scratch_shapes=scratch,
        compiler_params=_cparams(("parallel", "parallel", "arbitrary"), vmem_mib),
        name=name,
    )(*args)


def _mmh_body(*refs, n_extra, epilogue, hp):
    a_ref, w_ref = refs[0], refs[1]
    extras = [e[...] for e in refs[2:2 + n_extra]]
    o_ref = refs[2 + n_extra]
    a = a_ref[...]
    for hh in range(hp):
        acc = _dot(a, w_ref[hh])
        res = acc if epilogue is None else epilogue(acc, *extras)
        o_ref[hh] = res.astype(o_ref.dtype)


def _mm_heads(a, w, *, n_out, out_dtype, tm=1024, hp=4, extras=(), epilogue=None, name="mm_heads"):
    m, kdim = a.shape
    nh, _, n = w.shape
    tm = _tile(m, tm, 8)
    hp = _tile(nh, hp)
    in_specs = [pl.BlockSpec((tm, kdim), lambda i, h: (i, 0)),
                pl.BlockSpec((hp, kdim, n), lambda i, h: (h, 0, 0))]
    args = [a, w]
    for arr, bshape, imap in extras:
        in_specs.append(pl.BlockSpec(bshape, imap))
        args.append(arr)
    return pl.pallas_call(
        functools.partial(_mmh_body, n_extra=len(extras), epilogue=epilogue, hp=hp),
        grid=(m // tm, nh // hp),
        in_specs=in_specs,
        out_specs=pl.BlockSpec((hp, tm, n_out), lambda i, h: (h, i, 0)),
        out_shape=jax.ShapeDtypeStruct((nh, m, n_out), out_dtype),
        compiler_params=_cparams(("parallel", "arbitrary")),
        name=name,
    )(*args)


def _rms(x):
    return x * lax.rsqrt(jnp.mean(x * x, axis=-1, keepdims=True) + NORM_EPS)


def _norm_mod_body(*refs, n_prompt_tiles, with_residual):
    xp_ref, xs_ref, g_ref, sh_ref, sc_ref = refs[:5]
    refs = refs[5:]
    r_ref = None
    if with_residual:
        r_ref, refs = refs[0], refs[1:]
    o_ref = refs[0]
    tr, d = o_ref.shape

    def run(x_ref):
        x = x_ref[...]
        if with_residual:
            x = x + r_ref[...]
            refs[2][...] = x
        y = _rms(x) * g_ref[...]
        y = (y.reshape(tr // CHUNK, CHUNK, d) * (1.0 + sc_ref[...]) + sh_ref[...]).reshape(tr, d)
        o_ref[...] = y.astype(o_ref.dtype)
        if with_residual:
            refs[1][...] = y.T.astype(o_ref.dtype)

    i = pl.program_id(0)

    @pl.when(i < n_prompt_tiles)
    def _():
        run(xp_ref)

    @pl.when(i >= n_prompt_tiles)
    def _():
        run(xs_ref)


def _norm_mod(xp, xs, g, modg, shift_idx, scale_idx, *, tr, residual=None):
    tp, d = xp.shape
    ts = xs.shape[0]
    t = tp + ts
    npt = tp // tr
    gpt = tr // CHUNK
    with_residual = residual is not None
    row = lambda i: (i, 0)
    in_specs = [pl.BlockSpec((tr, d), lambda i: (jnp.minimum(i, npt - 1), 0)),
                pl.BlockSpec((tr, d), lambda i: (jnp.maximum(i - npt, 0), 0)),
                pl.BlockSpec((1, d), lambda i: (0, 0)),
                pl.BlockSpec((gpt, 1, d), lambda i: (i, 0, shift_idx)),
                pl.BlockSpec((gpt, 1, d), lambda i: (i, 0, scale_idx))]
    args = [xp, xs, g, modg, modg]
    out_specs = [pl.BlockSpec((tr, d), row)]
    out_shape = [jax.ShapeDtypeStruct((t, d), BF16)]
    if with_residual:
        in_specs.append(pl.BlockSpec((tr, d), row))
        args.append(residual)
        out_specs += [pl.BlockSpec((d, tr), lambda i: (0, i)), pl.BlockSpec((tr, d), row)]
        out_shape += [jax.ShapeDtypeStruct((d, t), BF16), jax.ShapeDtypeStruct((t, d), F32)]
    res = pl.pallas_call(
        functools.partial(_norm_mod_body, n_prompt_tiles=npt, with_residual=with_residual),
        grid=(t // tr,),
        in_specs=in_specs,
        out_specs=out_specs,
        out_shape=out_shape,
        compiler_params=_cparams(("arbitrary",), 56),
        name="norm_mod",
    )(*args)
    return res if with_residual else res[0]


def _rope64(x, cos, sin_signed):
    half = A_ROPE // 2
    rolled = jnp.concatenate([x[:, half:], x[:, :half]], axis=-1)
    return x * cos + rolled * sin_signed


def _log_sigmoid(x):
    return jnp.minimum(x, 0.0) - jnp.log1p(jnp.exp(-jnp.abs(x)))


def _small_post_body(z_ref, qg_ref, kvg_ref, cos_ref, sin_ref, bias_ref,
                     cq_ref, ckv_ref, kr_ref, gate_ref, *, qr, cr, mh):
    z = z_ref[...]
    cq_ref[...] = (_rms(z[:, :qr]) * qg_ref[...]).astype(cq_ref.dtype)
    ckv_ref[...] = _rms(z[:, qr:qr + cr]) * kvg_ref[...]
    o = qr + cr
    kr_ref[...] = _rope64(z[:, o:o + A_ROPE], cos_ref[...], sin_ref[...])
    o += A_ROPE
    pre = z[:, o:o + 2 * mh] + bias_ref[...]
    lane = lax.broadcasted_iota(jnp.int32, pre.shape, 1)
    gate_ref[...] = jnp.where(lane < mh, pre, _log_sigmoid(pre))


def _small_post(zs, qg, kvg, cos, sin_s, gate_bias, *, qr, cr, mh, tr):
    t, w = zs.shape
    row = lambda i: (i, 0)
    fix = lambda i: (0, 0)
    return pl.pallas_call(
        functools.partial(_small_post_body, qr=qr, cr=cr, mh=mh),
        grid=(t // tr,),
        in_specs=[pl.BlockSpec((tr, w), row), pl.BlockSpec((1, qr), fix), pl.BlockSpec((1, cr), fix),
                  pl.BlockSpec((tr, A_ROPE), row), pl.BlockSpec((tr, A_ROPE), row),
                  pl.BlockSpec((1, 2 * mh), fix)],
        out_specs=[pl.BlockSpec((tr, qr), row), pl.BlockSpec((tr, cr), row),
                   pl.BlockSpec((tr, A_ROPE), row), pl.BlockSpec((tr, 2 * mh), row)],
        out_shape=[jax.ShapeDtypeStruct((t, qr), BF16), jax.ShapeDtypeStruct((t, cr), F32),
                   jax.ShapeDtypeStruct((t, A_ROPE), F32), jax.ShapeDtypeStruct((t, 2 * mh), F32)],
        compiler_params=_cparams(("parallel",)),
        name="small_post",
    )(zs, qg, kvg, cos, sin_s, gate_bias)


def _mlstm_head(q, k, v, mo, ig_col, lf_col, g, c_prev, n_prev, m_prev, scale):
    L = q.shape[0]
    r = lax.broadcasted_iota(jnp.int32, (L, L), 0)
    c = lax.broadcasted_iota(jnp.int32, (L, L), 1)
    eye = r == c
    ig_row = jnp.sum(jnp.where(eye, ig_col, 0.0), axis=0, keepdims=True)
    lf_row = jnp.sum(jnp.where(eye, lf_col, 0.0), axis=0, keepdims=True)
    causal = c <= r
    b_col = jnp.sum(jnp.where(causal, lf_row, 0.0), axis=1, keepdims=True)
    b_row = jnp.sum(jnp.where(r <= c, lf_col, 0.0), axis=0, keepdims=True)
    b_last = jnp.sum(lf_col, axis=0, keepdims=True)

    a_col = b_col + m_prev
    dmat = jnp.where(causal, b_col - b_row + ig_row, NEG_INF)
    m_t = jnp.maximum(a_col, jnp.max(dmat, axis=1, keepdims=True))
    w_inter = jnp.exp(a_col - m_t)
    w_intra = jnp.exp(dmat - m_t)

    qs = q.astype(F32) * scale
    qb = qs.astype(BF16)
    s = _dot_nt(qb, k) * w_intra
    num = _dot(s.astype(BF16), v) + w_inter * _dot(qb, c_prev.astype(BF16))
    den = jnp.sum(s, axis=1, keepdims=True) + w_inter * jnp.sum(qs * n_prev, axis=1, keepdims=True)
    h = num / jnp.maximum(jnp.abs(den), jnp.exp(-m_t))
    h_out = _rms(h) * g * jax.nn.sigmoid(mo.astype(F32))

    m_new = jnp.sum(jnp.where(lax.broadcasted_iota(jnp.int32, (L, 1), 0) == L - 1, m_t, 0.0),
                    axis=0, keepdims=True)
    w_c = jnp.exp(b_last + m_prev - m_new)
    w_s = jnp.exp(b_last - b_col + ig_col - m_new)
    kw = k.astype(F32) * w_s
    c_new = w_c * c_prev + _dot(kw.T.astype(BF16), v)
    n_new = w_c * n_prev + jnp.sum(kw, axis=0, keepdims=True)
    return h_out, c_new, n_new, m_new


def _mlstm_body(*refs, has_state, mh, dk, dv, scale):
    if has_state:
        (q_ref, k_ref, v_ref, mo_ref, gate_ref, g_ref, c0_ref, n0_ref, m0_ref,
         h_ref, c_ref, n_ref, m_ref) = refs
    else:
        (q_ref, k_ref, v_ref, mo_ref, gate_ref, g_ref, h_ref, c_ref, n_ref, m_ref) = refs
    ci = pl.program_id(1)

    @pl.when(ci == 0)
    def _():
        if has_state:
            c_ref[...] = c0_ref[...]
            n_ref[...] = n0_ref[...]
            m_ref[...] = m0_ref[...]
        else:
            c_ref[...] = jnp.zeros_like(c_ref)
            n_ref[...] = jnp.zeros_like(n_ref)
            m_ref[...] = jnp.zeros_like(m_ref)

    gates = gate_ref[...]
    for hd in range(mh):
        h_out, c_new, n_new, m_new = _mlstm_head(
            q_ref[:, hd * dk:(hd + 1) * dk], k_ref[:, hd * dk:(hd + 1) * dk],
            v_ref[:, hd * dv:(hd + 1) * dv], mo_ref[:, hd * dv:(hd + 1) * dv],
            gates[:, hd:hd + 1], gates[:, mh + hd:mh + hd + 1], g_ref[:, hd * dv:(hd + 1) * dv],
            c_ref[0, hd], n_ref[0, hd], m_ref[0, hd], scale)
        h_ref[:, hd * dv:(hd + 1) * dv] = h_out.astype(h_ref.dtype)
        c_ref[0, hd] = c_new
        n_ref[0, hd] = n_new
        m_ref[0, hd] = m_new


def _mlstm(zb, gates, norm_g, state, *, row_blk0, nb, nc, mh, dk, dv, blk_q, blk_k, blk_v, blk_o):
    has_state = state is not None
    wq, wv = mh * dk, mh * dv
    rowmap = lambda blk: (lambda b, c: (row_blk0 + b * nc + c, blk))
    in_specs = [pl.BlockSpec((CHUNK, wq), rowmap(blk_q)),
                pl.BlockSpec((CHUNK, wq), rowmap(blk_k)),
                pl.BlockSpec((CHUNK, wv), rowmap(blk_v)),
                pl.BlockSpec((CHUNK, wv), rowmap(blk_o)),
                pl.BlockSpec((CHUNK, 2 * mh), rowmap(0)),
                pl.BlockSpec((1, wv), lambda b, c: (0, 0))]
    args = [zb, zb, zb, zb, gates, norm_g]
    st_specs = [pl.BlockSpec((1, mh, dk, dv), lambda b, c: (b, 0, 0, 0)),
                pl.BlockSpec((1, mh, 1, dk), lambda b, c: (b, 0, 0, 0)),
                pl.BlockSpec((1, mh, 1, 1), lambda b, c: (b, 0, 0, 0))]
    if has_state:
        in_specs += st_specs
        args += list(state)
    return pl.pallas_call(
        functools.partial(_mlstm_body, has_state=has_state, mh=mh, dk=dk, dv=dv, scale=dk ** -0.5),
        grid=(nb, nc),
        in_specs=in_specs,
        out_specs=[pl.BlockSpec((CHUNK, wv), lambda b, c: (b * nc + c, 0))] + st_specs,
        out_shape=[jax.ShapeDtypeStruct((nb * nc * CHUNK, wv), BF16),
                   jax.ShapeDtypeStruct((nb, mh, dk, dv), F32),
                   jax.ShapeDtypeStruct((nb, mh, 1, dk), F32),
                   jax.ShapeDtypeStruct((nb, mh, 1, 1), F32)],
        compiler_params=_cparams(("parallel", "arbitrary")),
        name="mlstm",
    )(*args)


def _attn_prompt_body(q_ref, ckv_ref, kr_ref, wuk_ref, wuv_ref, o_ref, k_scr, v_scr, *, tq, scale, hp):
    qi = pl.program_id(2)

    @pl.when(qi == 0)
    def _():
        ckv = ckv_ref[...].astype(BF16)
        kr = kr_ref[...].astype(BF16)
        for hh in range(hp):
            k_scr[hh, :, :A_NOPE] = _dot(ckv, wuk_ref[hh]).astype(BF16)
            k_scr[hh, :, A_NOPE:] = kr
            v_scr[hh] = _dot(ckv, wuv_ref[hh]).astype(BF16)

    def step(kb, carry, diagonal):
        rows = pl.ds(pl.multiple_of(kb * tq, tq), tq)
        out = []
        for hh in range(hp):
            m, l, acc = carry[hh]
            s = _dot_nt(q_ref[hh], k_scr[hh, rows, :]) * scale
            if diagonal:
                qc = lax.broadcasted_iota(jnp.int32, (tq, tq), 0) // CHUNK
                kc = lax.broadcasted_iota(jnp.int32, (tq, tq), 1) // CHUNK
                s = jnp.where(kc <= qc, s, NEG_INF)
            m_new = jnp.maximum(m, jnp.max(s, axis=1, keepdims=True))
            alpha = jnp.exp(m - m_new)
            p = jnp.exp(s - m_new)
            l = alpha * l + jnp.sum(p, axis=1, keepdims=True)
            acc = alpha * acc + _dot(p.astype(BF16), v_scr[hh, rows, :])
            out.append((m_new, l, acc))
        return tuple(out)

    init = tuple((jnp.full((tq, 1), NEG_INF, F32), jnp.zeros((tq, 1), F32), jnp.zeros((tq, A_DV), F32))
                 for _ in range(hp))
    carry = lax.fori_loop(0, qi, lambda kb, c: step(kb, c, False), init)
    final = step(qi, carry, True)
    for hh in range(hp):
        _, l, acc = final[hh]
        o_ref[:, hh * A_DV:(hh + 1) * A_DV] = (acc / l).astype(o_ref.dtype)


def _attn_prompt(q_hm, ckvn, krope, wuk_h, wuv_h, *, nb, seq, tq=512, hp=2):
    nh, _, dqk = q_hm.shape
    cr = ckvn.shape[1]
    tq = _tile(seq, tq, CHUNK)
    nq = seq // tq
    hp = _tile(nh, hp)
    return pl.pallas_call(
        functools.partial(_attn_prompt_body, tq=tq, scale=dqk ** -0.5, hp=hp),
        grid=(nb, nh // hp, nq),
        in_specs=[pl.BlockSpec((hp, tq, dqk), lambda b, h, i: (h, b * nq + i, 0)),
                  pl.BlockSpec((seq, cr), lambda b, h, i: (b, 0)),
                  pl.BlockSpec((seq, A_ROPE), lambda b, h, i: (b, 0)),
                  pl.BlockSpec((hp, cr, A_NOPE), lambda b, h, i: (h, 0, 0)),
                  pl.BlockSpec((hp, cr, A_DV), lambda b, h, i: (h, 0, 0))],
        out_specs=pl.BlockSpec((tq, hp * A_DV), lambda b, h, i: (b * nq + i, h)),
        out_shape=jax.ShapeDtypeStruct((nb * seq, nh * A_DV), BF16),
        scratch_shapes=[pltpu.VMEM((hp, seq, dqk), BF16), pltpu.VMEM((hp, seq, A_DV), BF16)],
        compiler_params=_cparams(("parallel", "arbitrary", "arbitrary")),
        name="attn_prompt",
    )(q_hm, ckvn, krope, wuk_h, wuv_h)


def _attn_sample_body(q_ref, cckv_ref, ckr_ref, nckv_ref, nkr_ref, wuk_ref, wuv_ref, o_ref, qlat_ref,
                      *, hg, past, scale, all_allowed):
    L = q_ref.shape[1]
    for hh in range(hg):
        qlat_ref[hh * L:(hh + 1) * L, :] = _dot_nt(q_ref[hh, :, :A_NOPE], wuk_ref[hh]).astype(BF16)
    qlat = qlat_ref[...]
    qr = q_ref[:, :, A_NOPE:].reshape(hg * L, A_ROPE)
    ckv_p = cckv_ref[0].astype(BF16)
    ckv_n = nckv_ref[...].astype(BF16)
    s_p = (_dot_nt(qlat, ckv_p) + _dot_nt(qr, ckr_ref[0].astype(BF16))) * scale
    s_n = (_dot_nt(qlat, ckv_n) + _dot_nt(qr, nkr_ref[...].astype(BF16))) * scale
    if not all_allowed:
        def masked(s, k0):
            qpos = past + lax.broadcasted_iota(jnp.int32, s.shape, 0) % L
            kpos = k0 + lax.broadcasted_iota(jnp.int32, s.shape, 1)
            return jnp.where((kpos // CHUNK) <= (qpos // CHUNK), s, NEG_INF)

        s_p = masked(s_p, 0)
        s_n = masked(s_n, past)
    m = jnp.maximum(jnp.max(s_p, axis=1, keepdims=True), jnp.max(s_n, axis=1, keepdims=True))
    p_p = jnp.exp(s_p - m)
    p_n = jnp.exp(s_n - m)
    l = jnp.sum(p_p, axis=1, keepdims=True) + jnp.sum(p_n, axis=1, keepdims=True)
    o = (_dot(p_p.astype(BF16), ckv_p) + _dot(p_n.astype(BF16), ckv_n)) / l
    ob = o.astype(BF16)
    for hh in range(hg):
        o_ref[:, hh * A_DV:(hh + 1) * A_DV] = _dot(ob[hh * L:(hh + 1) * L, :], wuv_ref[hh]).astype(o_ref.dtype)


def _attn_sample(q_hm, cache_ckv, cache_kr, ckvn, krope, wuk_h, wuv_h, *, row_blk0, nb, hg=8):
    nh = q_hm.shape[0]
    _, past, cr = cache_ckv.shape
    hg = _tile(nh, hg)
    dqk = A_NOPE + A_ROPE
    all_allowed = (past + CHUNK - 1) // CHUNK <= past // CHUNK
    return pl.pallas_call(
        functools.partial(_attn_sample_body, hg=hg, past=past, scale=dqk ** -0.5, all_allowed=all_allowed),
        grid=(nb, nh // hg),
        in_specs=[pl.BlockSpec((hg, CHUNK, dqk), lambda b, j: (j, row_blk0 + b, 0)),
                  pl.BlockSpec((1, past, cr), lambda b, j: (b, 0, 0)),
                  pl.BlockSpec((1, past, A_ROPE), lambda b, j: (b, 0, 0)),
                  pl.BlockSpec((CHUNK, cr), lambda b, j: (row_blk0 + b, 0)),
                  pl.BlockSpec((CHUNK, A_ROPE), lambda b, j: (row_blk0 + b, 0)),
                  pl.BlockSpec((hg, cr, A_NOPE), lambda b, j: (j, 0, 0)),
                  pl.BlockSpec((hg, cr, A_DV), lambda b, j: (j, 0, 0))],
        out_specs=pl.BlockSpec((CHUNK, hg * A_DV), lambda b, j: (b, j)),
        out_shape=jax.ShapeDtypeStruct((nb * CHUNK, nh * A_DV), BF16),
        scratch_shapes=[pltpu.VMEM((hg * CHUNK, cr), BF16)],
        compiler_params=_cparams(("parallel", "arbitrary")),
        name="attn_sample",
    )(q_hm, cache_ckv, cache_kr, ckvn, krope, wuk_h, wuv_h)


def _extract_topk(x, vals_ref, base):
    n = x.shape[0]
    row = lax.broadcasted_iota(jnp.int32, x.shape, 0)

    def body(it, x):
        m = jnp.max(x, axis=0, keepdims=True)
        vals_ref[pl.ds(base + it, 1), :] = m
        first = jnp.min(jnp.where(x == m, row, n), axis=0, keepdims=True)
        return jnp.where(row == first, NEG_INF, x)

    lax.fori_loop(0, PEER_TOPK, body, x)


def _cand_layout():
    kk = PEER_TOPK
    groups, a = [], 0
    while kk // (a + 1) > 1:
        nv = kk // (a + 1)
        groups.append((a, nv, -(-nv // SUBLANES) * SUBLANES))
        a += 1
    return groups, a


def _peer_topk_body(qp_ref, sk_ref, s1_ref, s2_ref, e1_ref, e2_ref, tau_ref, vals_ref, cand_ref, *, nh):
    kk = PEER_TOPK
    nkeys = PEER_NKEYS
    groups, tail = _cand_layout()
    for h in range(nh):
        s1 = _dot_nt(sk_ref[h, 0], qp_ref[:, (2 * h) * nkeys:(2 * h + 1) * nkeys])
        s2 = _dot_nt(sk_ref[h, 1], qp_ref[:, (2 * h + 1) * nkeys:(2 * h + 2) * nkeys])
        s1_ref[h] = s1
        s2_ref[h] = s2
        _extract_topk(s1, vals_ref, 0)
        _extract_topk(s2, vals_ref, kk)
        r0 = 0
        for a, nv, nr in groups:
            blk = vals_ref[a:a + 1, :] + vals_ref[kk:kk + nr, :]
            if nv < nr:
                blk = jnp.where(lax.broadcasted_iota(jnp.int32, blk.shape, 0) < nv, blk, NEG_INF)
            cand_ref[r0:r0 + nr, :] = blk
            r0 += nr
        cand_ref[r0:r0 + kk - tail, :] = vals_ref[tail:kk, :] + vals_ref[kk:kk + 1, :]
        max1 = vals_ref[0:1, :]
        max2 = vals_ref[kk:kk + 1, :]
        _extract_topk(cand_ref[...], vals_ref, 2 * kk)
        best = vals_ref[2 * kk:3 * kk, :]
        zsum = jnp.sum(jnp.exp(best - best[0:1, :]), axis=0, keepdims=True)
        tau_ref[h:h + 1, :] = best[kk - 1:kk, :]
        e1_ref[h] = jnp.exp(s1 - max1)
        e2_ref[h] = jnp.exp(s2 - max2) / zsum


def _peer_topk(qp, subkeys, *, tt):
    t = qp.shape[0]
    nh = subkeys.shape[0]
    nkeys = PEER_NKEYS
    groups, tail = _cand_layout()
    n_cand = sum(nr for _, _, nr in groups) + PEER_TOPK - tail
    stat = jax.ShapeDtypeStruct((nh, nkeys, t), F32)
    stat_spec = pl.BlockSpec((nh, nkeys, tt), lambda i: (0, 0, i))
    return pl.pallas_call(
        functools.partial(_peer_topk_body, nh=nh),
        grid=(t // tt,),
        in_specs=[pl.BlockSpec((tt, 2 * nh * nkeys), lambda i: (i, 0)),
                  pl.BlockSpec(subkeys.shape, lambda i: (0, 0, 0, 0))],
        out_specs=[stat_spec, stat_spec, stat_spec, stat_spec, pl.BlockSpec((nh, tt), lambda i: (0, i))],
        out_shape=[stat, stat, stat, stat, jax.ShapeDtypeStruct((nh, t), F32)],
        scratch_shapes=[pltpu.VMEM((4 * PEER_TOPK, tt), F32), pltpu.VMEM((n_cand, tt), F32)],
        compiler_params=_cparams(("parallel",)),
        name="peer_topk",
    )(qp, subkeys)


def _gelu(x):
    return 0.5 * x * (1.0 + lax.erf(x * (1.0 / math.sqrt(2.0))))


def _transpose_tiles_body(x_ref, o_ref):
    o_ref[0] = x_ref[...].T.astype(o_ref.dtype)


def _transpose_tiles(x, te):
    ne, d = x.shape
    return pl.pallas_call(
        _transpose_tiles_body,
        grid=(ne // te,),
        in_specs=[pl.BlockSpec((te, d), lambda i: (i, 0))],
        out_specs=pl.BlockSpec((1, d, te), lambda i: (i, 0, 0)),
        out_shape=jax.ShapeDtypeStruct((ne // te, d, te), BF16),
        compiler_params=_cparams(("parallel",)),
        name="transpose_tiles",
    )(x)


def _peer_dense_body(ht_ref, u_ref, vt_ref, s1_ref, s2_ref, e1_ref, e2_ref, tau_ref, o_ref,
                     act_a, act_b, coef_a, coef_b, *, nh, ni, n_et, crows):
    e = pl.program_id(1)
    nkeys = PEER_NKEYS
    d, tt = o_ref.shape
    te = act_a.shape[0]
    n_trips = te // crows
    dchunk = d // n_trips

    @pl.when(e == 0)
    def _():
        o_ref[...] = jnp.zeros_like(o_ref)
        act_b[...] = jnp.zeros_like(act_b)
        coef_b[...] = jnp.zeros_like(coef_b)

    tile_c = jnp.clip(e - 1, 0, n_et - 1)

    def step(act_w, act_r, coef_w, coef_r):
        act_w[...] = _dot(u_ref[...], ht_ref[...])

        for k in range(n_trips):
            r0 = k * dchunk
            o_ref[r0:r0 + dchunk, :] += _dot(vt_ref[0, r0:r0 + dchunk, :], coef_r[...])

            c0 = k * crows
            i = tile_c * ni + c0 // nkeys
            j0 = c0 % nkeys
            gate = None
            for h in range(nh):
                s1row = s1_ref[h, pl.ds(i, 1), :]
                e1row = e1_ref[h, pl.ds(i, 1), :]
                hit = (s1row + s2_ref[h, j0:j0 + crows, :]) >= tau_ref[h:h + 1, :]
                term = jnp.where(hit, e1row * e2_ref[h, j0:j0 + crows, :], 0.0)
                gate = term if gate is None else gate + term
            coef = gate * _gelu(act_r[c0:c0 + crows, :])
            coef_w[c0:c0 + crows, :] = coef.astype(coef_w.dtype)

    @pl.when(e % 2 == 0)
    def _():
        step(act_a, act_b, coef_a, coef_b)

    @pl.when(e % 2 == 1)
    def _():
        step(act_b, act_a, coef_b, coef_a)


def _peer_dense(h2_t, u_tab, vt_tiles, s1, s2, e1, e2, tau, *, tt, crows=32):
    d, t = h2_t.shape
    ne = u_tab.shape[0]
    nh = s1.shape[0]
    nkeys = PEER_NKEYS
    n_et, _, te = vt_tiles.shape
    ni = te // nkeys
    assert nkeys % crows == 0 and (d * crows) % (te * 16) == 0
    once = pl.Buffered(1)
    stat_spec = pl.BlockSpec((nh, nkeys, tt), lambda i, e: (0, 0, i), pipeline_mode=once)
    return pl.pallas_call(
        functools.partial(_peer_dense_body, nh=nh, ni=ni, n_et=n_et, crows=crows),
        grid=(t // tt, n_et + 2),
        in_specs=[pl.BlockSpec((d, tt), lambda i, e: (0, i), pipeline_mode=once),
                  pl.BlockSpec((te, d), lambda i, e: (jnp.minimum(e, n_et - 1), 0)),
                  pl.BlockSpec((1, d, te), lambda i, e: (jnp.clip(e - 2, 0, n_et - 1), 0, 0)),
                  stat_spec, stat_spec, stat_spec, stat_spec,
                  pl.BlockSpec((nh, tt), lambda i, e: (0, i), pipeline_mode=once)],
        out_specs=pl.BlockSpec((d, tt), lambda i, e: (0, i)),
        out_shape=jax.ShapeDtypeStruct((d, t), F32),
        scratch_shapes=[pltpu.VMEM((te, tt), F32), pltpu.VMEM((te, tt), F32),
                        pltpu.VMEM((te, tt), BF16), pltpu.VMEM((te, tt), BF16)],
        compiler_params=_cparams(("parallel", "arbitrary"), 56),
        name="peer_dense",
    )(h2_t, u_tab, vt_tiles, s1, s2, e1, e2, tau)


def _final_body(x_ref, pt_ref, g2_ref, gf_ref, o_ref):
    tr, d = x_ref.shape
    p = pt_ref[...].T
    x = x_ref[...].reshape(tr // CHUNK, CHUNK, d) + g2_ref[...] * p.reshape(tr // CHUNK, CHUNK, d)
    o_ref[...] = (_rms(x.reshape(tr, d)) * gf_ref[...]).astype(o_ref.dtype)


def _final(x1, pe_t, modg, gate_idx, gf, *, row_blk0, rows, tr):
    d = x1.shape[1]
    gpt = tr // CHUNK
    r0 = row_blk0 * CHUNK // tr
    return pl.pallas_call(
        _final_body,
        grid=(rows // tr,),
        in_specs=[pl.BlockSpec((tr, d), lambda i: (r0 + i, 0)),
                  pl.BlockSpec((d, tr), lambda i: (0, r0 + i)),
                  pl.BlockSpec((gpt, 1, d), lambda i: (r0 + i, 0, gate_idx)),
                  pl.BlockSpec((1, d), lambda i: (0, 0))],
        out_specs=pl.BlockSpec((tr, d), lambda i: (i, 0)),
        out_shape=jax.ShapeDtypeStruct((rows, d), F32),
        compiler_params=_cparams(("parallel",)),
        name="final_norm",
    )(x1, pe_t, modg, gf)


def _rope_tables(pos):
    half = A_ROPE // 2
    inv = ROPE_THETA ** (-jnp.arange(half, dtype=F32) / half)
    ang = pos.astype(F32)[:, None] * inv[None, :]
    cos, sin = jnp.cos(ang), jnp.sin(ang)
    return jnp.concatenate([cos, cos], axis=-1), jnp.concatenate([-sin, sin], axis=-1)


def _rope_q_epilogue(acc, cos, sin_s):
    return jnp.concatenate([acc[:, :A_NOPE], _rope64(acc[:, A_NOPE:], cos, sin_s)], axis=-1)


def kernel(x_prompt, x_sample, c_prompt, c_sample, cache_ckv, cache_krope, state_C, state_n, state_m, ada_w, ada_b, norm1_g, w_in, b_igate, b_fgate, mlstm_norm_g, q_norm_g, kv_norm_g, w_uq, w_uk, w_uv, w_branch_m, w_branch_a, w_out, norm2_g, peer_wq, peer_subkeys, peer_u, peer_v, final_norm_g):
    nbp, seq, d = x_prompt.shape
    nbs, dseq, _ = x_sample.shape
    depth = ada_w.shape[0]
    assert depth == 1 and seq % CHUNK == 0 and dseq == CHUNK
    past = cache_ckv.shape[2]
    mh, dk, dv = state_C.shape[2:]
    qr, ah, dqk = w_uq.shape[1:]
    cr = w_uk.shape[1]
    assert dqk == A_NOPE + A_ROPE and w_uv.shape[3] == A_DV and ah * A_DV == d and mh * dv == d
    assert (2 * dk) % dv == 0
    assert peer_subkeys.shape[3] == PEER_NKEYS and peer_subkeys.shape[4] == PEER_NKEYS
    tp, ts = nbp * seq, nbs * dseq
    t = tp + ts
    gp = tp // CHUNK
    ncp = seq // CHUNK

    def tok(want, mult=CHUNK):
        return _tile(math.gcd(tp, ts), want, mult)

    xp, xs = x_prompt.reshape(tp, d), x_sample.reshape(ts, d)
    pos = jnp.concatenate([jnp.tile(jnp.arange(seq), nbp), jnp.tile(past + jnp.arange(dseq), nbs)])
    cos_t, sin_t = _rope_tables(pos)

    nc_rows = nbp + nbs
    c_all = jnp.concatenate([c_prompt, c_sample], axis=0)
    c_pad = jnp.pad(c_all, ((0, (-nc_rows) % 16), (0, 0)))
    tn_ada = _tile(6 * d, 512, 128)
    mod = _mm(c_pad, ada_w.reshape(d, 6 * d), out_dtype=F32, tm=c_pad.shape[0], tn=tn_ada, tk=d, a_fn=_silu,
              extras=[(ada_b, (1, tn_ada), lambda i, j: (0, j))], epilogue=lambda acc, b: acc + b,
              name="adaln")
    modg = jnp.concatenate([jnp.repeat(mod[:nbp], ncp, axis=0), mod[nbp:nc_rows]], axis=0)
    modg = modg.reshape(t // CHUNK, 1, 6 * d)

    h1 = _norm_mod(xp, xs, norm1_g, modg, 0, 1, tr=tok(256))
    wi = w_in.reshape(d, w_in.shape[2])
    o_mi = 2 * mh * dk + 2 * mh * dv
    o_cq = o_mi + 2 * mh
    o_za = o_cq + qr + cr + A_ROPE
    ws_cols = qr + cr + A_ROPE + 2 * mh
    ws_pad = (-ws_cols) % 128
    w_small = jnp.concatenate([wi[:, o_cq:o_za], wi[:, o_mi:o_cq], jnp.zeros((d, ws_pad), F32)], axis=1).astype(BF16)
    zm = _mm(h1, wi[:, :o_mi].astype(BF16), out_dtype=BF16, tm=tok(1024), name="w_in_mlstm")
    zg = _mm(h1, wi[:, o_za:].astype(BF16), out_dtype=BF16, tm=tok(1024), name="w_in_gates")
    zs = _mm(h1, w_small, out_dtype=F32, tm=tok(512), tn=w_small.shape[1], name="w_in_small")
    wq_all, wv_all = mh * dk, mh * dv

    gate_bias = jnp.concatenate([b_igate, b_fgate], axis=1)
    cqn, ckvn, krope, gates = _small_post(zs, q_norm_g, kv_norm_g, cos_t, sin_t, gate_bias,
                                          qr=qr, cr=cr, mh=mh, tr=tok(256))

    mkw = dict(mh=mh, dk=dk, dv=dv, blk_q=0, blk_k=1, blk_v=2 * wq_all // wv_all, blk_o=2 * wq_all // wv_all + 1)
    hm_p, c_p, n_p, m_p = _mlstm(zm, gates, mlstm_norm_g, None, row_blk0=0, nb=nbp, nc=ncp, **mkw)
    st = (state_C[0], state_n[0].reshape(nbs, mh, 1, dk), state_m[0].reshape(nbs, mh, 1, 1))
    hm_s, c_s, n_s, m_s = _mlstm(zm, gates, mlstm_norm_g, st, row_blk0=gp, nb=nbs, nc=1, **mkw)
    hm = jnp.concatenate([hm_p, hm_s], axis=0)
    ya = _mm(hm, w_branch_m.reshape(d, d).astype(BF16), out_dtype=F32, tm=tok(1024), name="branch_m")

    wuq_h = jnp.transpose(w_uq[0], (1, 0, 2)).astype(BF16)
    wuk_h = jnp.transpose(w_uk[0], (1, 0, 2)).astype(BF16)
    wuv_h = jnp.transpose(w_uv[0], (1, 0, 2)).astype(BF16)
    tmh = tok(1024)
    q_hm = _mm_heads(cqn, wuq_h, n_out=dqk, out_dtype=BF16, tm=tmh, epilogue=_rope_q_epilogue,
                     extras=[(cos_t, (tmh, A_ROPE), lambda i, h: (i, 0)), (sin_t, (tmh, A_ROPE), lambda i, h: (i, 0))],
                     name="q_up")
    attn_p = _attn_prompt(q_hm, ckvn, krope, wuk_h, wuv_h, nb=nbp, seq=seq)
    attn_s = _attn_sample(q_hm, cache_ckv[0], cache_krope[0], ckvn, krope, wuk_h, wuv_h, row_blk0=gp, nb=nbs)
    attn = jnp.concatenate([attn_p, attn_s], axis=0)

    tm, tn = tok(512), _tile(d, 1024, 128)

    def merge(acc, ya_t, za_t, zb_t):
        return jax.nn.sigmoid(za_t.astype(F32)) * ya_t + jax.nn.sigmoid(zb_t.astype(F32)) * acc

    ym = _mm(attn, w_branch_a.reshape(d, d).astype(BF16), out_dtype=BF16, tm=tm, tn=tn, epilogue=merge,
             extras=[(ya, (tm, tn), lambda i, j: (i, j)),
                     (zg, (tm, tn), lambda i, j: (i, j)),
                     (zg, (tm, tn), lambda i, j: (i, d // tn + j))], name="branch_a_merge")

    def gated(acc, g_t):
        r, cdim = acc.shape
        return (g_t * acc.reshape(r // CHUNK, CHUNK, cdim)).reshape(r, cdim)

    r1 = _mm(ym, w_out.reshape(d, d).astype(BF16), out_dtype=F32, tm=tm, tn=tn, epilogue=gated,
             extras=[(modg, (tm // CHUNK, 1, tn), lambda i, j: (i, 0, 2 * (d // tn) + j))], name="w_out")

    h2, h2_t, x1 = _norm_mod(xp, xs, norm2_g, modg, 3, 4, tr=tok(256, 128), residual=r1)
    qp = _mm(h2, peer_wq.reshape(d, -1).astype(BF16), out_dtype=BF16, tm=tok(1024), name="peer_q")
    s1, s2, e1, e2, tau = _peer_topk(qp, peer_subkeys[0].astype(BF16), tt=tok(256, 128))
    ne = peer_u.shape[1]
    te = _tile(ne, 512, PEER_NKEYS)
    vt_tiles = _transpose_tiles(peer_v.reshape(ne, d), te)
    pe_t = _peer_dense(h2_t, peer_u.reshape(ne, d).astype(BF16), vt_tiles,
                       s1, s2, e1, e2, tau, tt=tok(512, 128))

    gf = final_norm_g.reshape(1, d)
    y_p = _final(x1, pe_t, modg, 5, gf, row_blk0=0, rows=tp, tr=tok(256, 128)).reshape(nbp, seq, d)
    y_s = _final(x1, pe_t, modg, 5, gf, row_blk0=gp, rows=ts, tr=tok(256, 128)).reshape(nbs, dseq, d)

    return (y_p, y_s,
            ckvn[:tp].reshape(1, nbp, seq, cr), krope[:tp].reshape(1, nbp, seq, A_ROPE),
            c_p[None], n_p.reshape(1, nbp, mh, dk), m_p.reshape(1, nbp, mh),
            ckvn[tp:].reshape(1, nbs, dseq, cr), krope[tp:].reshape(1, nbs, dseq, A_ROPE),
            c_s[None], n_s.reshape(1, nbs, mh, dk), m_s.reshape(1, nbs, mh))
```

```python
import functools
import math

import jax
import jax.numpy as jnp
from jax import lax
from jax.experimental import pallas as pl
from jax.experimental.pallas import tpu as pltpu

CHUNK = 64
A_NOPE = 128
A_ROPE = 64
A_DV = 128
ROPE_THETA = 10000.0
PEER_NKEYS = 128
PEER_TOPK = 16
NORM_EPS = 1e-6

F32 = jnp.float32
BF16 = jnp.bfloat16
MIB = 1024 * 1024
NEG_INF = float("-inf")
SUBLANES = 8


def _cparams(sem, vmem_mib=48):
    return pltpu.CompilerParams(dimension_semantics=sem, vmem_limit_bytes=vmem_mib * MIB)


def _tile(n, want, mult=1):
    if n <= want:
        return n
    t = (want // mult) * mult
    while t >= mult:
        if n % t == 0:
            return t
        t -= mult
    return n


def _dot(a, b):
    return jnp.dot(a, b, preferred_element_type=F32)


def _dot_nt(a, b):
    return lax.dot_general(a, b, (((1,), (1,)), ((), ())), preferred_element_type=F32)


def _silu(x):
    return x * jax.nn.sigmoid(x)


def _mm_body(*refs, nk, n_extra, a_fn, epilogue, n_first):
    if n_first is not None:
        a2_ref, refs = refs[1], refs[:1] + refs[2:]
    a_ref, w_ref = refs[0], refs[1]
    extras = refs[2:2 + n_extra]
    o_ref = refs[2 + n_extra]

    def finish(acc):
        res = acc if epilogue is None else epilogue(acc, *[e[...] for e in extras])
        o_ref[...] = res.astype(o_ref.dtype)

    def product(ref):
        a = ref[...]
        if a_fn is not None:
            a = a_fn(a)
        return _dot(a.astype(BF16), w_ref[...].astype(BF16))

    if n_first is not None:
        i = pl.program_id(0)

        @pl.when(i < n_first)
        def _():
            finish(product(a_ref))

        @pl.when(i >= n_first)
        def _():
            finish(product(a2_ref))

        return

    part = product(a_ref)
    if nk == 1:
        finish(part)
        return
    acc_ref = refs[3 + n_extra]
    k = pl.program_id(2)

    @pl.when(k == 0)
    def _():
        acc_ref[...] = part

    @pl.when(k > 0)
    def _():
        acc_ref[...] += part

    @pl.when(k == nk - 1)
    def _():
        finish(acc_ref[...])


def _mm(a, w, *, out_dtype, tm=1024, tn=1024, tk=4096, extras=(), epilogue=None, a_fn=None,
        vmem_mib=48, name="mm"):
    parts = a if isinstance(a, tuple) else (a,)
    m, kdim = sum(p.shape[0] for p in parts), parts[0].shape[1]
    n_out = w.shape[1]
    tm = _tile(m, tm, 8)
    tn = _tile(n_out, tn, 128)
    tk = _tile(kdim, tk, 128)
    nk = kdim // tk
    grid = (m // tm, n_out // tn, nk)
    n_first = None
    if len(parts) == 2:
        assert nk == 1 and parts[0].shape[0] % tm == 0 and parts[1].shape[0] % tm == 0
        n_first = parts[0].shape[0] // tm
        in_specs = [pl.BlockSpec((tm, tk), lambda i, j, k: (jnp.minimum(i, n_first - 1), k)),
                    pl.BlockSpec((tm, tk), lambda i, j, k: (jnp.maximum(i - n_first, 0), k))]
    else:
        in_specs = [pl.BlockSpec((tm, tk), lambda i, j, k: (i, k))]
    in_specs.append(pl.BlockSpec((tk, tn), lambda i, j, k: (k, j)))
    args = [*parts, w]
    for arr, bshape, imap in extras:
        in_specs.append(pl.BlockSpec(bshape, functools.partial(lambda i, j, k, f: f(i, j), f=imap)))
        args.append(arr)
    scratch = [pltpu.VMEM((tm, tn), F32)] if nk > 1 else []
    return pl.pallas_call(
        functools.partial(_mm_body, nk=nk, n_extra=len(extras), a_fn=a_fn, epilogue=epilogue, n_first=n_first),
        grid=grid,
        in_specs=in_specs,
        out_specs=pl.BlockSpec((tm, tn), lambda i, j, k: (i, j)),
        out_shape=jax.ShapeDtypeStruct((m, n_out), out_dtype),
        scratch_shapes=scratch,
        compiler_params=_cparams(("parallel", "parallel", "arbitrary"), vmem_mib),
        name=name,
    )(*args)


def _mmh_body(*refs, n_extra, epilogue, hp):
    a_ref, w_ref = refs[0], refs[1]
    extras = [e[...] for e in refs[2:2 + n_extra]]
    o_ref = refs[2 + n_extra]
    a = a_ref[...]
    for hh in range(hp):
        acc = _dot(a, w_ref[hh])
        res = acc if epilogue is None else epilogue(acc, *extras)
        o_ref[hh] = res.astype(o_ref.dtype)


def _mm_heads(a, w, *, n_out, out_dtype, tm=1024, hp=4, extras=(), epilogue=None, name="mm_heads"):
    m, kdim = a.shape
    nh, _, n = w.shape
    tm = _tile(m, tm, 8)
    hp = _tile(nh, hp)
    in_specs = [pl.BlockSpec((tm, kdim), lambda i, h: (i, 0)),
                pl.BlockSpec((hp, kdim, n), lambda i, h: (h, 0, 0))]
    args = [a, w]
    for arr, bshape, imap in extras:
        in_specs.append(pl.BlockSpec(bshape, imap))
        args.append(arr)
    return pl.pallas_call(
        functools.partial(_mmh_body, n_extra=len(extras), epilogue=epilogue, hp=hp),
        grid=(m // tm, nh // hp),
        in_specs=in_specs,
        out_specs=pl.BlockSpec((hp, tm, n_out), lambda i, h: (h, i, 0)),
        out_shape=jax.ShapeDtypeStruct((nh, m, n_out), out_dtype),
        compiler_params=_cparams(("parallel", "arbitrary")),
        name=name,
    )(*args)


def _rms(x):
    return x * lax.rsqrt(jnp.mean(x * x, axis=-1, keepdims=True) + NORM_EPS)


def _norm_mod_body(*refs, n_prompt_tiles, with_residual):
    xp_ref, xs_ref, g_ref, sh_ref, sc_ref = refs[:5]
    refs = refs[5:]
    r_ref = None
    if with_residual:
        r_ref, refs = refs[0], refs[1:]
    o_ref = refs[0]
    tr, d = o_ref.shape

    def run(x_ref):
        x = x_ref[...]
        if with_residual:
            x = x + r_ref[...]
            refs[2][...] = x
        y = _rms(x) * g_ref[...]
        y = (y.reshape(tr // CHUNK, CHUNK, d) * (1.0 + sc_ref[...]) + sh_ref[...]).reshape(tr, d)
        o_ref[...] = y.astype(o_ref.dtype)
        if with_residual:
            refs[1][...] = y.T.astype(o_ref.dtype)

    i = pl.program_id(0)

    @pl.when(i < n_prompt_tiles)
    def _():
        run(xp_ref)

    @pl.when(i >= n_prompt_tiles)
    def _():
        run(xs_ref)


def _norm_mod(xp, xs, g, modg, shift_idx, scale_idx, *, tr, residual=None):
    tp, d = xp.shape
    ts = xs.shape[0]
    t = tp + ts
    npt = tp // tr
    gpt = tr // CHUNK
    with_residual = residual is not None
    row = lambda i: (i, 0)
    in_specs = [pl.BlockSpec((tr, d), lambda i: (jnp.minimum(i, npt - 1), 0)),
                pl.BlockSpec((tr, d), lambda i: (jnp.maximum(i - npt, 0), 0)),
                pl.BlockSpec((1, d), lambda i: (0, 0)),
                pl.BlockSpec((gpt, 1, d), lambda i: (i, 0, shift_idx)),
                pl.BlockSpec((gpt, 1, d), lambda i: (i, 0, scale_idx))]
    args = [xp, xs, g, modg, modg]
    out_specs = [pl.BlockSpec((tr, d), row)]
    out_shape = [jax.ShapeDtypeStruct((t, d), BF16)]
    if with_residual:
        in_specs.append(pl.BlockSpec((tr, d), row))
        args.append(residual)
        out_specs += [pl.BlockSpec((d, tr), lambda i: (0, i)), pl.BlockSpec((tr, d), row)]
        out_shape += [jax.ShapeDtypeStruct((d, t), BF16), jax.ShapeDtypeStruct((t, d), F32)]
    res = pl.pallas_call(
        functools.partial(_norm_mod_body, n_prompt_tiles=npt, with_residual=with_residual),
        grid=(t // tr,),
        in_specs=in_specs,
        out_specs=out_specs,
        out_shape=out_shape,
        compiler_params=_cparams(("arbitrary",), 56),
        name="norm_mod",
    )(*args)
    return res if with_residual else res[0]


def _rope64(x, cos, sin_signed):
    half = A_ROPE // 2
    rolled = jnp.concatenate([x[:, half:], x[:, :half]], axis=-1)
    return x * cos + rolled * sin_signed


def _log_sigmoid(x):
    return jnp.minimum(x, 0.0) - jnp.log1p(jnp.exp(-jnp.abs(x)))


def _small_post_body(z_ref, qg_ref, kvg_ref, cos_ref, sin_ref, bias_ref,
                     cq_ref, ckv_ref, kr_ref, gate_ref, *, qr, cr, mh):
    z = z_ref[...]
    cq_ref[...] = (_rms(z[:, :qr]) * qg_ref[...]).astype(cq_ref.dtype)
    ckv_ref[...] = _rms(z[:, qr:qr + cr]) * kvg_ref[...]
    o = qr + cr
    kr_ref[...] = _rope64(z[:, o:o + A_ROPE], cos_ref[...], sin_ref[...])
    o += A_ROPE
    pre = z[:, o:o + 2 * mh] + bias_ref[...]
    lane = lax.broadcasted_iota(jnp.int32, pre.shape, 1)
    gate_ref[...] = jnp.where(lane < mh, pre, _log_sigmoid(pre))


def _small_post(zs, qg, kvg, cos, sin_s, gate_bias, *, qr, cr, mh, tr):
    t, w = zs.shape
    row = lambda i: (i, 0)
    fix = lambda i: (0, 0)
    return pl.pallas_call(
        functools.partial(_small_post_body, qr=qr, cr=cr, mh=mh),
        grid=(t // tr,),
        in_specs=[pl.BlockSpec((tr, w), row), pl.BlockSpec((1, qr), fix), pl.BlockSpec((1, cr), fix),
                  pl.BlockSpec((tr, A_ROPE), row), pl.BlockSpec((tr, A_ROPE), row),
                  pl.BlockSpec((1, 2 * mh), fix)],
        out_specs=[pl.BlockSpec((tr, qr), row), pl.BlockSpec((tr, cr), row),
                   pl.BlockSpec((tr, A_ROPE), row), pl.BlockSpec((tr, 2 * mh), row)],
        out_shape=[jax.ShapeDtypeStruct((t, qr), BF16), jax.ShapeDtypeStruct((t, cr), F32),
                   jax.ShapeDtypeStruct((t, A_ROPE), F32), jax.ShapeDtypeStruct((t, 2 * mh), F32)],
        compiler_params=_cparams(("parallel",)),
        name="small_post",
    )(zs, qg, kvg, cos, sin_s, gate_bias)


def _mlstm_head(q, k, v, mo, ig_col, lf_col, g, c_prev, n_prev, m_prev, scale):
    L = q.shape[0]
    r = lax.broadcasted_iota(jnp.int32, (L, L), 0)
    c = lax.broadcasted_iota(jnp.int32, (L, L), 1)
    eye = r == c
    ig_row = jnp.sum(jnp.where(eye, ig_col, 0.0), axis=0, keepdims=True)
    lf_row = jnp.sum(jnp.where(eye, lf_col, 0.0), axis=0, keepdims=True)
    causal = c <= r
    b_col = jnp.sum(jnp.where(causal, lf_row, 0.0), axis=1, keepdims=True)
    b_row = jnp.sum(jnp.where(r <= c, lf_col, 0.0), axis=0, keepdims=True)
    b_last = jnp.sum(lf_col, axis=0, keepdims=True)

    a_col = b_col + m_prev
    dmat = jnp.where(causal, b_col - b_row + ig_row, NEG_INF)
    m_t = jnp.maximum(a_col, jnp.max(dmat, axis=1, keepdims=True))
    w_inter = jnp.exp(a_col - m_t)
    w_intra = jnp.exp(dmat - m_t)

    qs = q.astype(F32) * scale
    qb = qs.astype(BF16)
    s = _dot_nt(qb, k) * w_intra
    num = _dot(s.astype(BF16), v) + w_inter * _dot(qb, c_prev.astype(BF16))
    den = jnp.sum(s, axis=1, keepdims=True) + w_inter * jnp.sum(qs * n_prev, axis=1, keepdims=True)
    h = num / jnp.maximum(jnp.abs(den), jnp.exp(-m_t))
    h_out = _rms(h) * g * jax.nn.sigmoid(mo.astype(F32))

    m_new = jnp.sum(jnp.where(lax.broadcasted_iota(jnp.int32, (L, 1), 0) == L - 1, m_t, 0.0),
                    axis=0, keepdims=True)
    w_c = jnp.exp(b_last + m_prev - m_new)
    w_s = jnp.exp(b_last - b_col + ig_col - m_new)
    kw = k.astype(F32) * w_s
    c_new = w_c * c_prev + _dot(kw.T.astype(BF16), v)
    n_new = w_c * n_prev + jnp.sum(kw, axis=0, keepdims=True)
    return h_out, c_new, n_new, m_new


def _mlstm_body(*refs, has_state, mh, dk, dv, scale):
    if has_state:
        (q_ref, k_ref, v_ref, mo_ref, gate_ref, g_ref, c0_ref, n0_ref, m0_ref,
         h_ref, c_ref, n_ref, m_ref) = refs
    else:
        (q_ref, k_ref, v_ref, mo_ref, gate_ref, g_ref, h_ref, c_ref, n_ref, m_ref) = refs
    ci = pl.program_id(1)

    @pl.when(ci == 0)
    def _():
        if has_state:
            c_ref[...] = c0_ref[...]
            n_ref[...] = n0_ref[...]
            m_ref[...] = m0_ref[...]
        else:
            c_ref[...] = jnp.zeros_like(c_ref)
            n_ref[...] = jnp.zeros_like(n_ref)
            m_ref[...] = jnp.zeros_like(m_ref)

    gates = gate_ref[...]
    for hd in range(mh):
        h_out, c_new, n_new, m_new = _mlstm_head(
            q_ref[:, hd * dk:(hd + 1) * dk], k_ref[:, hd * dk:(hd + 1) * dk],
            v_ref[:, hd * dv:(hd + 1) * dv], mo_ref[:, hd * dv:(hd + 1) * dv],
            gates[:, hd:hd + 1], gates[:, mh + hd:mh + hd + 1], g_ref[:, hd * dv:(hd + 1) * dv],
            c_ref[0, hd], n_ref[0, hd], m_ref[0, hd], scale)
        h_ref[:, hd * dv:(hd + 1) * dv] = h_out.astype(h_ref.dtype)
        c_ref[0, hd] = c_new
        n_ref[0, hd] = n_new
        m_ref[0, hd] = m_new


def _mlstm(zb, gates, norm_g, state, *, row_blk0, nb, nc, mh, dk, dv, blk_q, blk_k, blk_v, blk_o):
    has_state = state is not None
    wq, wv = mh * dk, mh * dv
    rowmap = lambda blk: (lambda b, c: (row_blk0 + b * nc + c, blk))
    in_specs = [pl.BlockSpec((CHUNK, wq), rowmap(blk_q)),
                pl.BlockSpec((CHUNK, wq), rowmap(blk_k)),
                pl.BlockSpec((CHUNK, wv), rowmap(blk_v)),
                pl.BlockSpec((CHUNK, wv), rowmap(blk_o)),
                pl.BlockSpec((CHUNK, 2 * mh), rowmap(0)),
                pl.BlockSpec((1, wv), lambda b, c: (0, 0))]
    args = [zb, zb, zb, zb, gates, norm_g]
    st_specs = [pl.BlockSpec((1, mh, dk, dv), lambda b, c: (b, 0, 0, 0)),
                pl.BlockSpec((1, mh, 1, dk), lambda b, c: (b, 0, 0, 0)),
                pl.BlockSpec((1, mh, 1, 1), lambda b, c: (b, 0, 0, 0))]
    if has_state:
        in_specs += st_specs
        args += list(state)
    return pl.pallas_call(
        functools.partial(_mlstm_body, has_state=has_state, mh=mh, dk=dk, dv=dv, scale=dk ** -0.5),
        grid=(nb, nc),
        in_specs=in_specs,
        out_specs=[pl.BlockSpec((CHUNK, wv), lambda b, c: (b * nc + c, 0))] + st_specs,
        out_shape=[jax.ShapeDtypeStruct((nb * nc * CHUNK, wv), BF16),
                   jax.ShapeDtypeStruct((nb, mh, dk, dv), F32),
                   jax.ShapeDtypeStruct((nb, mh, 1, dk), F32),
                   jax.ShapeDtypeStruct((nb, mh, 1, 1), F32)],
        compiler_params=_cparams(("parallel", "arbitrary")),
        name="mlstm",
    )(*args)


def _attn_prompt_body(q_ref, ckv_ref, kr_ref, wuk_ref, wuv_ref, o_ref, k_scr, v_scr, *, tq, scale, hp):
    qi = pl.program_id(2)

    @pl.when(qi == 0)
    def _():
        ckv = ckv_ref[...].astype(BF16)
        kr = kr_ref[...].astype(BF16)
        for hh in range(hp):
            k_scr[hh, :, :A_NOPE] = _dot(ckv, wuk_ref[hh]).astype(BF16)
            k_scr[hh, :, A_NOPE:] = kr
            v_scr[hh] = _dot(ckv, wuv_ref[hh]).astype(BF16)

    def step(kb, carry, diagonal):
        rows = pl.ds(pl.multiple_of(kb * tq, tq), tq)
        out = []
        for hh in range(hp):
            m, l, acc = carry[hh]
            s = _dot_nt(q_ref[hh], k_scr[hh, rows, :]) * scale
            if diagonal:
                qc = lax.broadcasted_iota(jnp.int32, (tq, tq), 0) // CHUNK
                kc = lax.broadcasted_iota(jnp.int32, (tq, tq), 1) // CHUNK
                s = jnp.where(kc <= qc, s, NEG_INF)
            m_new = jnp.maximum(m, jnp.max(s, axis=1, keepdims=True))
            alpha = jnp.exp(m - m_new)
            p = jnp.exp(s - m_new)
            l = alpha * l + jnp.sum(p, axis=1, keepdims=True)
            acc = alpha * acc + _dot(p.astype(BF16), v_scr[hh, rows, :])
            out.append((m_new, l, acc))
        return tuple(out)

    init = tuple((jnp.full((tq, 1), NEG_INF, F32), jnp.zeros((tq, 1), F32), jnp.zeros((tq, A_DV), F32))
                 for _ in range(hp))
    carry = lax.fori_loop(0, qi, lambda kb, c: step(kb, c, False), init)
    final = step(qi, carry, True)
    for hh in range(hp):
        _, l, acc = final[hh]
        o_ref[:, hh * A_DV:(hh + 1) * A_DV] = (acc / l).astype(o_ref.dtype)


def _attn_prompt(q_hm, ckvn, krope, wuk_h, wuv_h, *, nb, seq, tq=512, hp=2):
    nh, _, dqk = q_hm.shape
    cr = ckvn.shape[1]
    tq = _tile(seq, tq, CHUNK)
    nq = seq // tq
    hp = _tile(nh, hp)
    return pl.pallas_call(
        functools.partial(_attn_prompt_body, tq=tq, scale=dqk ** -0.5, hp=hp),
        grid=(nb, nh // hp, nq),
        in_specs=[pl.BlockSpec((hp, tq, dqk), lambda b, h, i: (h, b * nq + i, 0)),
                  pl.BlockSpec((seq, cr), lambda b, h, i: (b, 0)),
                  pl.BlockSpec((seq, A_ROPE), lambda b, h, i: (b, 0)),
                  pl.BlockSpec((hp, cr, A_NOPE), lambda b, h, i: (h, 0, 0)),
                  pl.BlockSpec((hp, cr, A_DV), lambda b, h, i: (h, 0, 0))],
        out_specs=pl.BlockSpec((tq, hp * A_DV), lambda b, h, i: (b * nq + i, h)),
        out_shape=jax.ShapeDtypeStruct((nb * seq, nh * A_DV), BF16),
        scratch_shapes=[pltpu.VMEM((hp, seq, dqk), BF16), pltpu.VMEM((hp, seq, A_DV), BF16)],
        compiler_params=_cparams(("parallel", "arbitrary", "arbitrary")),
        name="attn_prompt",
    )(q_hm, ckvn, krope, wuk_h, wuv_h)


def _attn_sample_body(q_ref, cckv_ref, ckr_ref, nckv_ref, nkr_ref, wuk_ref, wuv_ref, o_ref, qlat_ref,
                      *, hg, past, scale, all_allowed):
    L = q_ref.shape[1]
    for hh in range(hg):
        qlat_ref[hh * L:(hh + 1) * L, :] = _dot_nt(q_ref[hh, :, :A_NOPE], wuk_ref[hh]).astype(BF16)
    qlat = qlat_ref[...]
    qr = q_ref[:, :, A_NOPE:].reshape(hg * L, A_ROPE)
    ckv_p = cckv_ref[0].astype(BF16)
    ckv_n = nckv_ref[...].astype(BF16)
    s_p = (_dot_nt(qlat, ckv_p) + _dot_nt(qr, ckr_ref[0].astype(BF16))) * scale
    s_n = (_dot_nt(qlat, ckv_n) + _dot_nt(qr, nkr_ref[...].astype(BF16))) * scale
    if not all_allowed:
        def masked(s, k0):
            qpos = past + lax.broadcasted_iota(jnp.int32, s.shape, 0) % L
            kpos = k0 + lax.broadcasted_iota(jnp.int32, s.shape, 1)
            return jnp.where((kpos // CHUNK) <= (qpos // CHUNK), s, NEG_INF)

        s_p = masked(s_p, 0)
        s_n = masked(s_n, past)
    m = jnp.maximum(jnp.max(s_p, axis=1, keepdims=True), jnp.max(s_n, axis=1, keepdims=True))
    p_p = jnp.exp(s_p - m)
    p_n = jnp.exp(s_n - m)
    l = jnp.sum(p_p, axis=1, keepdims=True) + jnp.sum(p_n, axis=1, keepdims=True)
    o = (_dot(p_p.astype(BF16), ckv_p) + _dot(p_n.astype(BF16), ckv_n)) / l
    ob = o.astype(BF16)
    for hh in range(hg):
        o_ref[:, hh * A_DV:(hh + 1) * A_DV] = _dot(ob[hh * L:(hh + 1) * L, :], wuv_ref[hh]).astype(o_ref.dtype)


def _attn_sample(q_hm, cache_ckv, cache_kr, ckvn, krope, wuk_h, wuv_h, *, row_blk0, nb, hg=8):
    nh = q_hm.shape[0]
    _, past, cr = cache_ckv.shape
    hg = _tile(nh, hg)
    dqk = A_NOPE + A_ROPE
    all_allowed = (past + CHUNK - 1) // CHUNK <= past // CHUNK
    return pl.pallas_call(
        functools.partial(_attn_sample_body, hg=hg, past=past, scale=dqk ** -0.5, all_allowed=all_allowed),
        grid=(nb, nh // hg),
        in_specs=[pl.BlockSpec((hg, CHUNK, dqk), lambda b, j: (j, row_blk0 + b, 0)),
                  pl.BlockSpec((1, past, cr), lambda b, j: (b, 0, 0)),
                  pl.BlockSpec((1, past, A_ROPE), lambda b, j: (b, 0, 0)),
                  pl.BlockSpec((CHUNK, cr), lambda b, j: (row_blk0 + b, 0)),
                  pl.BlockSpec((CHUNK, A_ROPE), lambda b, j: (row_blk0 + b, 0)),
                  pl.BlockSpec((hg, cr, A_NOPE), lambda b, j: (j, 0, 0)),
                  pl.BlockSpec((hg, cr, A_DV), lambda b, j: (j, 0, 0))],
        out_specs=pl.BlockSpec((CHUNK, hg * A_DV), lambda b, j: (b, j)),
        out_shape=jax.ShapeDtypeStruct((nb * CHUNK, nh * A_DV), BF16),
        scratch_shapes=[pltpu.VMEM((hg * CHUNK, cr), BF16)],
        compiler_params=_cparams(("parallel", "arbitrary")),
        name="attn_sample",
    )(q_hm, cache_ckv, cache_kr, ckvn, krope, wuk_h, wuv_h)


def _extract_topk(x, vals_ref, base):
    n = x.shape[0]
    row = lax.broadcasted_iota(jnp.int32, x.shape, 0)

    def body(it, x):
        m = jnp.max(x, axis=0, keepdims=True)
        vals_ref[pl.ds(base + it, 1), :] = m
        first = jnp.min(jnp.where(x == m, row, n), axis=0, keepdims=True)
        return jnp.where(row == first, NEG_INF, x)

    lax.fori_loop(0, PEER_TOPK, body, x)


def _cand_layout():
    kk = PEER_TOPK
    groups, a = [], 0
    while kk // (a + 1) > 1:
        nv = kk // (a + 1)
        groups.append((a, nv, -(-nv // SUBLANES) * SUBLANES))
        a += 1
    return groups, a


def _peer_topk_body(qp_ref, sk_ref, s1_ref, s2_ref, e1_ref, e2_ref, tau_ref, vals_ref, cand_ref, *, nh):
    kk = PEER_TOPK
    nkeys = PEER_NKEYS
    groups, tail = _cand_layout()
    for h in range(nh):
        s1 = _dot_nt(sk_ref[h, 0], qp_ref[:, (2 * h) * nkeys:(2 * h + 1) * nkeys])
        s2 = _dot_nt(sk_ref[h, 1], qp_ref[:, (2 * h + 1) * nkeys:(2 * h + 2) * nkeys])
        s1_ref[h] = s1
        s2_ref[h] = s2
        _extract_topk(s1, vals_ref, 0)
        _extract_topk(s2, vals_ref, kk)
        r0 = 0
        for a, nv, nr in groups:
            blk = vals_ref[a:a + 1, :] + vals_ref[kk:kk + nr, :]
            if nv < nr:
                blk = jnp.where(lax.broadcasted_iota(jnp.int32, blk.shape, 0) < nv, blk, NEG_INF)
            cand_ref[r0:r0 + nr, :] = blk
            r0 += nr
        cand_ref[r0:r0 + kk - tail, :] = vals_ref[tail:kk, :] + vals_ref[kk:kk + 1, :]
        max1 = vals_ref[0:1, :]
        max2 = vals_ref[kk:kk + 1, :]
        _extract_topk(cand_ref[...], vals_ref, 2 * kk)
        best = vals_ref[2 * kk:3 * kk, :]
        zsum = jnp.sum(jnp.exp(best - best[0:1, :]), axis=0, keepdims=True)
        tau_ref[h:h + 1, :] = best[kk - 1:kk, :]
        e1_ref[h] = jnp.exp(s1 - max1)
        e2_ref[h] = jnp.exp(s2 - max2) / zsum


def _peer_topk(qp, subkeys, *, tt):
    t = qp.shape[0]
    nh = subkeys.shape[0]
    nkeys = PEER_NKEYS
    groups, tail = _cand_layout()
    n_cand = sum(nr for _, _, nr in groups) + PEER_TOPK - tail
    stat = jax.ShapeDtypeStruct((nh, nkeys, t), F32)
    stat_spec = pl.BlockSpec((nh, nkeys, tt), lambda i: (0, 0, i))
    return pl.pallas_call(
        functools.partial(_peer_topk_body, nh=nh),
        grid=(t // tt,),
        in_specs=[pl.BlockSpec((tt, 2 * nh * nkeys), lambda i: (i, 0)),
                  pl.BlockSpec(subkeys.shape, lambda i: (0, 0, 0, 0))],
        out_specs=[stat_spec, stat_spec, stat_spec, stat_spec, pl.BlockSpec((nh, tt), lambda i: (0, i))],
        out_shape=[stat, stat, stat, stat, jax.ShapeDtypeStruct((nh, t), F32)],
        scratch_shapes=[pltpu.VMEM((4 * PEER_TOPK, tt), F32), pltpu.VMEM((n_cand, tt), F32)],
        compiler_params=_cparams(("parallel",)),
        name="peer_topk",
    )(qp, subkeys)


def _gelu(x):
    return 0.5 * x * (1.0 + lax.erf(x * (1.0 / math.sqrt(2.0))))


def _transpose_tiles_body(x_ref, o_ref):
    o_ref[0] = x_ref[...].T.astype(o_ref.dtype)


def _transpose_tiles(x, te):
    ne, d = x.shape
    return pl.pallas_call(
        _transpose_tiles_body,
        grid=(ne // te,),
        in_specs=[pl.BlockSpec((te, d), lambda i: (i, 0))],
        out_specs=pl.BlockSpec((1, d, te), lambda i: (i, 0, 0)),
        out_shape=jax.ShapeDtypeStruct((ne // te, d, te), BF16),
        compiler_params=_cparams(("parallel",)),
        name="transpose_tiles",
    )(x)


def _peer_dense_body(ht_ref, u_ref, vt_ref, s1_ref, s2_ref, e1_ref, e2_ref, tau_ref, o_ref,
                     act_a, act_b, coef_a, coef_b, *, nh, ni, n_et, n_tiles, crows):
    e = pl.program_id(0)
    nkeys = PEER_NKEYS
    d, tt = o_ref.shape
    te = act_a.shape[0]
    n_trips = te // crows
    dchunk = d // n_trips

    @pl.when(e == 0)
    def _():
        act_b[...] = jnp.zeros_like(act_b)
        coef_b[...] = jnp.zeros_like(coef_b)

    @pl.when(jnp.clip(e - 2, 0, n_tiles - 1) % n_et == 0)
    def _():
        o_ref[...] = jnp.zeros_like(o_ref)

    tile_c = jnp.clip(e - 1, 0, n_tiles - 1) % n_et

    def step(act_w, act_r, coef_w, coef_r):
        act_w[...] = _dot(u_ref[...], ht_ref[...])

        for k in range(n_trips):
            r0 = k * dchunk
            o_ref[r0:r0 + dchunk, :] += _dot(vt_ref[0, r0:r0 + dchunk, :], coef_r[...])

            c0 = k * crows
            i = tile_c * ni + c0 // nkeys
            j0 = c0 % nkeys
            gate = None
            for h in range(nh):
                s1row = s1_ref[h, pl.ds(i, 1), :]
                e1row = e1_ref[h, pl.ds(i, 1), :]
                hit = (s1row + s2_ref[h, j0:j0 + crows, :]) >= tau_ref[h:h + 1, :]
                term = jnp.where(hit, e1row * e2_ref[h, j0:j0 + crows, :], 0.0)
                gate = term if gate is None else gate + term
            coef = gate * _gelu(act_r[c0:c0 + crows, :])
            coef_w[c0:c0 + crows, :] = coef.astype(coef_w.dtype)

    @pl.when(e % 2 == 0)
    def _():
        step(act_a, act_b, coef_a, coef_b)

    @pl.when(e % 2 == 1)
    def _():
        step(act_b, act_a, coef_b, coef_a)


def _peer_dense(h2_t, u_tab, vt_tiles, s1, s2, e1, e2, tau, *, tt, crows=32):
    d, t = h2_t.shape
    ne = u_tab.shape[0]
    nh = s1.shape[0]
    nkeys = PEER_NKEYS
    n_et, _, te = vt_tiles.shape
    ni = te // nkeys
    assert nkeys % crows == 0 and (d * crows) % (te * 16) == 0
    once = pl.Buffered(1)
    n_tiles = (t // tt) * n_et
    stage = lambda lag: (lambda e: jnp.clip(e - lag, 0, n_tiles - 1))
    act_t, coef_t, drain_t = stage(0), stage(1), stage(2)
    stat_spec = pl.BlockSpec((nh, nkeys, tt), lambda e: (0, 0, coef_t(e) // n_et), pipeline_mode=once)
    return pl.pallas_call(
        functools.partial(_peer_dense_body, nh=nh, ni=ni, n_et=n_et, n_tiles=n_tiles, crows=crows),
        grid=(n_tiles + 2,),
        in_specs=[pl.BlockSpec((d, tt), lambda e: (0, act_t(e) // n_et), pipeline_mode=once),
                  pl.BlockSpec((te, d), lambda e: (act_t(e) % n_et, 0)),
                  pl.BlockSpec((1, d, te), lambda e: (drain_t(e) % n_et, 0, 0)),
                  stat_spec, stat_spec, stat_spec, stat_spec,
                  pl.BlockSpec((nh, tt), lambda e: (0, coef_t(e) // n_et), pipeline_mode=once)],
        out_specs=pl.BlockSpec((d, tt), lambda e: (0, drain_t(e) // n_et)),
        out_shape=jax.ShapeDtypeStruct((d, t), F32),
        scratch_shapes=[pltpu.VMEM((te, tt), F32), pltpu.VMEM((te, tt), F32),
                        pltpu.VMEM((te, tt), BF16), pltpu.VMEM((te, tt), BF16)],
        compiler_params=_cparams(("arbitrary",), 56),
        name="peer_dense",
    )(h2_t, u_tab, vt_tiles, s1, s2, e1, e2, tau)


def _final_body(x_ref, pt_ref, g2_ref, gf_ref, o_ref):
    tr, d = x_ref.shape
    p = pt_ref[...].T
    x = x_ref[...].reshape(tr // CHUNK, CHUNK, d) + g2_ref[...] * p.reshape(tr // CHUNK, CHUNK, d)
    o_ref[...] = (_rms(x.reshape(tr, d)) * gf_ref[...]).astype(o_ref.dtype)


def _final(x1, pe_t, modg, gate_idx, gf, *, row_blk0, rows, tr):
    d = x1.shape[1]
    gpt = tr // CHUNK
    r0 = row_blk0 * CHUNK // tr
    return pl.pallas_call(
        _final_body,
        grid=(rows // tr,),
        in_specs=[pl.BlockSpec((tr, d), lambda i: (r0 + i, 0)),
                  pl.BlockSpec((d, tr), lambda i: (0, r0 + i)),
                  pl.BlockSpec((gpt, 1, d), lambda i: (r0 + i, 0, gate_idx)),
                  pl.BlockSpec((1, d), lambda i: (0, 0))],
        out_specs=pl.BlockSpec((tr, d), lambda i: (i, 0)),
        out_shape=jax.ShapeDtypeStruct((rows, d), F32),
        compiler_params=_cparams(("parallel",)),
        name="final_norm",
    )(x1, pe_t, modg, gf)


def _rope_tables(pos):
    half = A_ROPE // 2
    inv = ROPE_THETA ** (-jnp.arange(half, dtype=F32) / half)
    ang = pos.astype(F32)[:, None] * inv[None, :]
    cos, sin = jnp.cos(ang), jnp.sin(ang)
    return jnp.concatenate([cos, cos], axis=-1), jnp.concatenate([-sin, sin], axis=-1)


def _rope_q_epilogue(acc, cos, sin_s):
    return jnp.concatenate([acc[:, :A_NOPE], _rope64(acc[:, A_NOPE:], cos, sin_s)], axis=-1)


def kernel(x_prompt, x_sample, c_prompt, c_sample, cache_ckv, cache_krope, state_C, state_n, state_m, ada_w, ada_b, norm1_g, w_in, b_igate, b_fgate, mlstm_norm_g, q_norm_g, kv_norm_g, w_uq, w_uk, w_uv, w_branch_m, w_branch_a, w_out, norm2_g, peer_wq, peer_subkeys, peer_u, peer_v, final_norm_g):
    nbp, seq, d = x_prompt.shape
    nbs, dseq, _ = x_sample.shape
    depth = ada_w.shape[0]
    assert depth == 1 and seq % CHUNK == 0 and dseq == CHUNK
    past = cache_ckv.shape[2]
    mh, dk, dv = state_C.shape[2:]
    qr, ah, dqk = w_uq.shape[1:]
    cr = w_uk.shape[1]
    assert dqk == A_NOPE + A_ROPE and w_uv.shape[3] == A_DV and ah * A_DV == d and mh * dv == d
    assert (2 * dk) % dv == 0
    assert peer_subkeys.shape[3] == PEER_NKEYS and peer_subkeys.shape[4] == PEER_NKEYS
    tp, ts = nbp * seq, nbs * dseq
    t = tp + ts
    gp = tp // CHUNK
    ncp = seq // CHUNK

    def tok(want, mult=CHUNK):
        return _tile(math.gcd(tp, ts), want, mult)

    xp, xs = x_prompt.reshape(tp, d), x_sample.reshape(ts, d)
    pos = jnp.concatenate([jnp.tile(jnp.arange(seq), nbp), jnp.tile(past + jnp.arange(dseq), nbs)])
    cos_t, sin_t = _rope_tables(pos)

    nc_rows = nbp + nbs
    c_all = jnp.concatenate([c_prompt, c_sample], axis=0)
    c_pad = jnp.pad(c_all, ((0, (-nc_rows) % 16), (0, 0)))
    tn_ada = _tile(6 * d, 512, 128)
    mod = _mm(c_pad, ada_w.reshape(d, 6 * d), out_dtype=F32, tm=c_pad.shape[0], tn=tn_ada, tk=d, a_fn=_silu,
              extras=[(ada_b, (1, tn_ada), lambda i, j: (0, j))], epilogue=lambda acc, b: acc + b,
              name="adaln")
    modg = jnp.concatenate([jnp.repeat(mod[:nbp], ncp, axis=0), mod[nbp:nc_rows]], axis=0)
    modg = modg.reshape(t // CHUNK, 1, 6 * d)

    h1 = _norm_mod(xp, xs, norm1_g, modg, 0, 1, tr=tok(256))
    wi = w_in.reshape(d, w_in.shape[2])
    o_mi = 2 * mh * dk + 2 * mh * dv
    o_cq = o_mi + 2 * mh
    o_za = o_cq + qr + cr + A_ROPE
    ws_cols = qr + cr + A_ROPE + 2 * mh
    ws_pad = (-ws_cols) % 128
    w_small = jnp.concatenate([wi[:, o_cq:o_za], wi[:, o_mi:o_cq], jnp.zeros((d, ws_pad), F32)], axis=1).astype(BF16)
    zm = _mm(h1, wi[:, :o_mi].astype(BF16), out_dtype=BF16, tm=tok(1024), name="w_in_mlstm")
    zg = _mm(h1, wi[:, o_za:].astype(BF16), out_dtype=BF16, tm=tok(1024), name="w_in_gates")
    zs = _mm(h1, w_small, out_dtype=F32, tm=tok(512), tn=w_small.shape[1], name="w_in_small")
    wq_all, wv_all = mh * dk, mh * dv

    gate_bias = jnp.concatenate([b_igate, b_fgate], axis=1)
    cqn, ckvn, krope, gates = _small_post(zs, q_norm_g, kv_norm_g, cos_t, sin_t, gate_bias,
                                          qr=qr, cr=cr, mh=mh, tr=tok(256))

    mkw = dict(mh=mh, dk=dk, dv=dv, blk_q=0, blk_k=1, blk_v=2 * wq_all // wv_all, blk_o=2 * wq_all // wv_all + 1)
    hm_p, c_p, n_p, m_p = _mlstm(zm, gates, mlstm_norm_g, None, row_blk0=0, nb=nbp, nc=ncp, **mkw)
    st = (state_C[0], state_n[0].reshape(nbs, mh, 1, dk), state_m[0].reshape(nbs, mh, 1, 1))
    hm_s, c_s, n_s, m_s = _mlstm(zm, gates, mlstm_norm_g, st, row_blk0=gp, nb=nbs, nc=1, **mkw)
    ya = _mm((hm_p, hm_s), w_branch_m.reshape(d, d).astype(BF16), out_dtype=F32, tm=tok(512), name="branch_m")

    wuq_h = jnp.transpose(w_uq[0], (1, 0, 2)).astype(BF16)
    wuk_h = jnp.transpose(w_uk[0], (1, 0, 2)).astype(BF16)
    wuv_h = jnp.transpose(w_uv[0], (1, 0, 2)).astype(BF16)
    tmh = tok(1024)
    q_hm = _mm_heads(cqn, wuq_h, n_out=dqk, out_dtype=BF16, tm=tmh, epilogue=_rope_q_epilogue,
                     extras=[(cos_t, (tmh, A_ROPE), lambda i, h: (i, 0)), (sin_t, (tmh, A_ROPE), lambda i, h: (i, 0))],
                     name="q_up")
    attn_p = _attn_prompt(q_hm, ckvn, krope, wuk_h, wuv_h, nb=nbp, seq=seq)
    attn_s = _attn_sample(q_hm, cache_ckv[0], cache_krope[0], ckvn, krope, wuk_h, wuv_h, row_blk0=gp, nb=nbs)

    tm, tn = tok(512), _tile(d, 1024, 128)

    def merge(acc, ya_t, za_t, zb_t):
        return jax.nn.sigmoid(za_t.astype(F32)) * ya_t + jax.nn.sigmoid(zb_t.astype(F32)) * acc

    ym = _mm((attn_p, attn_s), w_branch_a.reshape(d, d).astype(BF16), out_dtype=BF16, tm=tm, tn=tn, epilogue=merge,
             extras=[(ya, (tm, tn), lambda i, j: (i, j)),
                     (zg, (tm, tn), lambda i, j: (i, j)),
                     (zg, (tm, tn), lambda i, j: (i, d // tn + j))], name="branch_a_merge")

    def gated(acc, g_t):
        r, cdim = acc.shape
        return (g_t * acc.reshape(r // CHUNK, CHUNK, cdim)).reshape(r, cdim)

    r1 = _mm(ym, w_out.reshape(d, d).astype(BF16), out_dtype=F32, tm=tm, tn=tn, epilogue=gated,
             extras=[(modg, (tm // CHUNK, 1, tn), lambda i, j: (i, 0, 2 * (d // tn) + j))], name="w_out")

    h2, h2_t, x1 = _norm_mod(xp, xs, norm2_g, modg, 3, 4, tr=tok(256, 128), residual=r1)
    qp = _mm(h2, peer_wq.reshape(d, -1).astype(BF16), out_dtype=BF16, tm=tok(1024), name="peer_q")
    s1, s2, e1, e2, tau = _peer_topk(qp, peer_subkeys[0].astype(BF16), tt=tok(256, 128))
    ne = peer_u.shape[1]
    te = _tile(ne, 512, PEER_NKEYS)
    vt_tiles = _transpose_tiles(peer_v.reshape(ne, d), te)
    pe_t = _peer_dense(h2_t, peer_u.reshape(ne, d).astype(BF16), vt_tiles,
                       s1, s2, e1, e2, tau, tt=tok(512, 128))

    gf = final_norm_g.reshape(1, d)
    y_p = _final(x1, pe_t, modg, 5, gf, row_blk0=0, rows=tp, tr=tok(256, 128)).reshape(nbp, seq, d)
    y_s = _final(x1, pe_t, modg, 5, gf, row_blk0=gp, rows=ts, tr=tok(256, 128)).reshape(nbs, dseq, d)

    return (y_p, y_s,
            ckvn[:tp].reshape(1, nbp, seq, cr), krope[:tp].reshape(1, nbp, seq, A_ROPE),
            c_p[None], n_p.reshape(1, nbp, mh, dk), m_p.reshape(1, nbp, mh),
            ckvn[tp:].reshape(1, nbs, dseq, cr), krope[tp:].reshape(1, nbs, dseq, A_ROPE),
            c_s[None], n_s.reshape(1, nbs, mh, dk), m_s.reshape(1, nbs, mh))
```
